```python
import jax, jax.numpy as jnp
from jax import lax
import numpy as np

D_MODEL = 1024
BATCH = 8
SEQ = 2048
DEPTH = 4

N_MIXERS = 3
RMS_EPS = 1e-6
RW_HEAD = 64
RW_HEADS = D_MODEL // RW_HEAD
RW_DECAY_LORA = 64
RW_AAA_LORA = 64
RW_GATE_LORA = 128
RW_LNX_EPS = 64e-5
RW_N_MIX = 6
SC_WIDTH = 3
LRU_WIDTH = D_MODEL
LRU_HEADS = 4
LRU_BLOCK = LRU_WIDTH // LRU_HEADS
LRU_CONV = 4
LRU_C = 8.0
N_EXPERTS = 16
EXPERT_FF = 2 * D_MODEL
CAPACITY_FACTOR = 2
N_LAYERS_A = (DEPTH + N_MIXERS - 1) // N_MIXERS
N_LAYERS_B = (DEPTH + N_MIXERS - 2) // N_MIXERS
N_LAYERS_C = DEPTH // N_MIXERS

kernel_name = "hybrid_rwkv7_shortconv_rglru_ecmoe_encoder"


def rms_norm(x, g):
    xf = x.astype(jnp.float32)
    y = xf * lax.rsqrt(jnp.mean(xf * xf, axis=-1, keepdims=True) + RMS_EPS)
    return (y * g.astype(jnp.float32)).astype(x.dtype)


def centred_shift(x):
    prev = jnp.pad(x[:, :-1], ((0, 0), (1, 0), (0, 0)))
    nxt = jnp.pad(x[:, 1:], ((0, 0), (0, 1), (0, 0)))
    return 0.5 * (prev + nxt)


def depthwise_conv(x, w, pad_left, pad_right):
    return lax.conv_general_dilated(
        x, w.astype(x.dtype)[:, None, :], window_strides=(1,),
        padding=[(pad_left, pad_right)], dimension_numbers=('NWC', 'WIO', 'NWC'),
        feature_group_count=x.shape[-1])


def rwkv7_scan(r, w, k, v, kk, a, reverse):
    bsz, _, nh, n = r.shape

    def step(state, inp):
        r_t, w_t, k_t, v_t, kk_t, a_t = inp
        sa = jnp.einsum('bhvk,bhk->bhv', state, kk_t)
        state = (state * w_t[:, :, None, :]
                 - sa[..., None] * (kk_t * a_t)[:, :, None, :]
                 + v_t[..., None] * k_t[:, :, None, :])
        return state, jnp.einsum('bhvk,bhk->bhv', state, r_t)

    xs = tuple(jnp.moveaxis(t, 1, 0) for t in (r, w, k, v, kk, a))
    s0 = jnp.zeros((bsz, nh, n, n), jnp.float32)
    _, ys = lax.scan(step, s0, xs, reverse=reverse)
    return jnp.moveaxis(ys, 0, 1)


def rwkv7_mixer(h, mu, w_in, w0, w1, w2, a0, a1, a2, g1, g2, k_k, k_a, r_k, lnx_g, lnx_b, w_out):
    bsz, seq, d = h.shape
    f32 = jnp.float32
    xx = centred_shift(h) - h
    xm = h[None] + xx[None] * mu[:, None, None, :]
    r, k, v = jnp.einsum('pbsd,pde->pbse', xm[:3], w_in)
    x_w, x_a, x_g = xm[3], xm[4], xm[5]
    lora_w = jnp.einsum('zbsr,zrd->zbsd', jnp.tanh(jnp.einsum('bsd,zdr->zbsr', x_w, w1)), w2)
    w_log = -jax.nn.softplus(-(w0[:, None, None, :] + lora_w).astype(f32)) - 0.5
    decay = jnp.exp(-jnp.exp(w_log))
    lora_a = jnp.einsum('zbsr,zrd->zbsd', jnp.einsum('bsd,zdr->zbsr', x_a, a1), a2)
    a = jax.nn.sigmoid((a0[:, None, None, :] + lora_a).astype(f32))
    g = jnp.einsum('bsr,rd->bsd', jax.nn.sigmoid(jnp.einsum('bsd,dr->bsr', x_g, g1)), g2)

    def heads(t):
        return t.reshape(t.shape[:-1] + (RW_HEADS, RW_HEAD))

    r_h = heads(r.astype(f32))
    v_h = heads(v.astype(f32))
    kk = heads((k * k_k).astype(f32))
    kk = kk / jnp.maximum(jnp.sqrt(jnp.sum(kk * kk, axis=-1, keepdims=True)), 1e-12)
    k_dir = heads(k.astype(f32)[None] * (1.0 + (a - 1.0) * k_a.astype(f32)))
    a_h = heads(a)
    decay_h = heads(decay)
    y = (rwkv7_scan(r_h, decay_h[0], k_dir[0], v_h, kk, a_h[0], reverse=False)
         + rwkv7_scan(r_h, decay_h[1], k_dir[1], v_h, kk, a_h[1], reverse=True))
    mean = jnp.mean(y, axis=-1, keepdims=True)
    var = jnp.mean(jnp.square(y - mean), axis=-1, keepdims=True)
    y = ((y - mean) * lax.rsqrt(var + RW_LNX_EPS)).reshape(bsz, seq, d)
    y = y * lnx_g.astype(f32) + lnx_b.astype(f32)
    bonus = jnp.sum(r_h[None] * k_dir * r_k.astype(f32), axis=-1, keepdims=True) * v_h[None]
    y = y + jnp.sum(bonus, axis=0).reshape(bsz, seq, d)
    return jnp.einsum('bsd,de->bse', y.astype(h.dtype) * g, w_out)


def short_conv_mixer(h, w_in, conv_w, w_out):
    bcu = jnp.einsum('bsd,de->bse', h, w_in)
    b_gate, c_gate, u = jnp.split(bcu, 3, axis=-1)
    z = depthwise_conv(c_gate * u, conv_w, 1, 1)
    return jnp.einsum('bse,ed->bsd', b_gate * z, w_out)


def linear_scan(a, b):
    def combine(e1, e2):
        a1, b1 = e1
        a2, b2 = e2
        return a1 * a2, a2 * b1 + b2
    _, hs = lax.associative_scan(combine, (a, b), axis=1)
    return hs


def rglru_mixer(h, w_in, conv_w, conv_b, w_a, b_a, w_x, b_x, lam, w_out):
    bsz, seq, _ = h.shape
    f32 = jnp.float32
    gu = jnp.einsum('bsd,de->bse', h, w_in)
    gate, u = jnp.split(gu, 2, axis=-1)
    gate = jax.nn.gelu(gate)
    u = depthwise_conv(u, conv_w, 2, 1) + conv_b
    uh = u.reshape(bsz, seq, LRU_HEADS, LRU_BLOCK)
    r_gate = jax.nn.sigmoid((jnp.einsum('bshi,zhij->zbshj', uh, w_a).reshape(2, bsz, seq, LRU_WIDTH)
                             + b_a[:, None, None, :]).astype(f32))
    i_gate = jax.nn.sigmoid((jnp.einsum('bshi,zhij->zbshj', uh, w_x).reshape(2, bsz, seq, LRU_WIDTH)
                             + b_x[:, None, None, :]).astype(f32))
    log_a = -LRU_C * r_gate * jax.nn.softplus(-lam.astype(f32))[:, None, None, :]
    a = jnp.exp(log_a)
    b = jnp.sqrt(-jnp.expm1(2.0 * log_a)) * (i_gate * u.astype(f32)[None])
    h_fwd = linear_scan(a[0], b[0])
    h_bwd = jnp.flip(linear_scan(jnp.flip(a[1], axis=1), jnp.flip(b[1], axis=1)), axis=1)
    y = (h_fwd + h_bwd).astype(h.dtype) * gate
    return jnp.einsum('bse,ed->bsd', y, w_out)


def expert_choice_moe(h, router_w, w_gate, w_up, w_down):
    bsz, seq, _ = h.shape
    cap = CAPACITY_FACTOR * seq // N_EXPERTS
    logits = jnp.einsum('bsd,de->bse', h, router_w).astype(jnp.float32)
    aff = jax.nn.softmax(logits, axis=-1)
    gates, idx = lax.top_k(jnp.swapaxes(aff, 1, 2), cap)
    bidx = jnp.arange(bsz)[:, None, None]
    xin = h[bidx, idx]
    hid = (jax.nn.silu(jnp.einsum('becd,edf->becf', xin, w_gate))
           * jnp.einsum('becd,edf->becf', xin, w_up))
    yexp = jnp.einsum('becf,efd->becd', hid, w_down) * gates[..., None].astype(h.dtype)
    return jnp.zeros_like(h).at[bidx, idx].add(yexp)


def setup_inputs(seed: int = 0) -> dict:
    key = jax.random.key(seed)
    keys = iter(jax.random.split(key, 48))
    f32 = jnp.float32
    D, E, F = D_MODEL, N_EXPERTS, EXPERT_FF

    def nrm(shape, scale):
        return scale * jax.random.normal(next(keys), shape, f32)

    def uni(shape, lo, hi):
        return jax.random.uniform(next(keys), shape, f32, minval=lo, maxval=hi)

    inp = {}
    inp['x'] = nrm((BATCH, SEQ, D), 1.0)
    inp['norm_mix_g'] = 1.0 + nrm((DEPTH, D), 0.02)
    inp['norm_ffn_g'] = 1.0 + nrm((DEPTH, D), 0.02)
    inp['router_w'] = nrm((DEPTH, D, E), D ** -0.5)
    inp['exp_w_gate'] = nrm((DEPTH, E, D, F), D ** -0.5)
    inp['exp_w_up'] = nrm((DEPTH, E, D, F), D ** -0.5)
    inp['exp_w_down'] = nrm((DEPTH, E, F, D), F ** -0.5)
    inp['final_norm_g'] = 1.0 + nrm((D,), 0.02)
    inp['rw_mu'] = uni((N_LAYERS_A, RW_N_MIX, D), 0.0, 1.0)
    inp['rw_w_in'] = nrm((N_LAYERS_A, 3, D, D), D ** -0.5)
    inp['rw_w0'] = uni((N_LAYERS_A, 2, D), -7.0, 0.0)
    inp['rw_w1'] = nrm((N_LAYERS_A, 2, D, RW_DECAY_LORA), D ** -0.5)
    inp['rw_w2'] = nrm((N_LAYERS_A, 2, RW_DECAY_LORA, D), 0.1 * RW_DECAY_LORA ** -0.5)
    inp['rw_a0'] = nrm((N_LAYERS_A, 2, D), 0.5)
    inp['rw_a1'] = nrm((N_LAYERS_A, 2, D, RW_AAA_LORA), D ** -0.5)
    inp['rw_a2'] = nrm((N_LAYERS_A, 2, RW_AAA_LORA, D), 0.1 * RW_AAA_LORA ** -0.5)
    inp['rw_g1'] = nrm((N_LAYERS_A, D, RW_GATE_LORA), D ** -0.5)
    inp['rw_g2'] = nrm((N_LAYERS_A, RW_GATE_LORA, D), RW_GATE_LORA ** -0.5)
    inp['rw_k_k'] = 0.85 + nrm((N_LAYERS_A, D), 0.02)
    inp['rw_k_a'] = 1.0 + nrm((N_LAYERS_A, D), 0.02)
    inp['rw_r_k'] = nrm((N_LAYERS_A, RW_HEADS, RW_HEAD), 0.1)
    inp['rw_lnx_g'] = 1.0 + nrm((N_LAYERS_A, D), 0.02)
    inp['rw_lnx_b'] = nrm((N_LAYERS_A, D), 0.02)
    inp['rw_w_out'] = nrm((N_LAYERS_A, D, D), D ** -0.5)
    inp['sc_w_in'] = nrm((N_LAYERS_B, D, 3 * D), D ** -0.5)
    inp['sc_conv_w'] = nrm((N_LAYERS_B, SC_WIDTH, D), SC_WIDTH ** -0.5)
    inp['sc_w_out'] = nrm((N_LAYERS_B, D, D), D ** -0.5)
    inp['lru_w_in'] = nrm((N_LAYERS_C, D, 2 * LRU_WIDTH), D ** -0.5)
    inp['lru_conv_w'] = nrm((N_LAYERS_C, LRU_CONV, LRU_WIDTH), LRU_CONV ** -0.5)
    inp['lru_conv_b'] = nrm((N_LAYERS_C, LRU_WIDTH), 0.01)
    inp['lru_w_a'] = nrm((N_LAYERS_C, 2, LRU_HEADS, LRU_BLOCK, LRU_BLOCK), LRU_BLOCK ** -0.5)
    inp['lru_b_a'] = nrm((N_LAYERS_C, 2, LRU_WIDTH), 0.01)
    inp['lru_w_x'] = nrm((N_LAYERS_C, 2, LRU_HEADS, LRU_BLOCK, LRU_BLOCK), LRU_BLOCK ** -0.5)
    inp['lru_b_x'] = nrm((N_LAYERS_C, 2, LRU_WIDTH), 0.01)
    a_c = uni((N_LAYERS_C, 2, LRU_WIDTH), 0.9, 0.999)
    s = a_c ** (1.0 / LRU_C)
    inp['lru_lambda'] = jnp.log(s) - jnp.log1p(-s)
    inp['lru_w_out'] = nrm((N_LAYERS_C, LRU_WIDTH, D), LRU_WIDTH ** -0.5)
    return inp


def reference(x, norm_mix_g, norm_ffn_g, router_w, exp_w_gate, exp_w_up, exp_w_down, final_norm_g,
              rw_mu, rw_w_in, rw_w0, rw_w1, rw_w2, rw_a0, rw_a1, rw_a2, rw_g1, rw_g2,
              rw_k_k, rw_k_a, rw_r_k, rw_lnx_g, rw_lnx_b, rw_w_out,
              sc_w_in, sc_conv_w, sc_w_out,
              lru_w_in, lru_conv_w, lru_conv_b, lru_w_a, lru_b_a, lru_w_x, lru_b_x, lru_lambda, lru_w_out):
    ia, ib, ic = 0, 0, 0
    for i in range(DEPTH):
        h = rms_norm(x, norm_mix_g[i])
        m = i % N_MIXERS
        if m == 0:
            y = rwkv7_mixer(h, rw_mu[ia], rw_w_in[ia], rw_w0[ia], rw_w1[ia], rw_w2[ia],
                            rw_a0[ia], rw_a1[ia], rw_a2[ia], rw_g1[ia], rw_g2[ia],
                            rw_k_k[ia], rw_k_a[ia], rw_r_k[ia], rw_lnx_g[ia], rw_lnx_b[ia], rw_w_out[ia])
            ia += 1
        elif m == 1:
            y = short_conv_mixer(h, sc_w_in[ib], sc_conv_w[ib], sc_w_out[ib])
            ib += 1
        else:
            y = rglru_mixer(h, lru_w_in[ic], lru_conv_w[ic], lru_conv_b[ic], lru_w_a[ic], lru_b_a[ic],
                            lru_w_x[ic], lru_b_x[ic], lru_lambda[ic], lru_w_out[ic])
            ic += 1
        x = x + y
        h = rms_norm(x, norm_ffn_g[i])
        x = x + expert_choice_moe(h, router_w[i], exp_w_gate[i], exp_w_up[i], exp_w_down[i])
    return rms_norm(x, final_norm_g)
```

```python
import functools

import jax
import jax.numpy as jnp
from jax import lax
from jax.experimental import pallas as pl
from jax.experimental.pallas import tpu as pltpu

F32 = jnp.float32
BF16 = jnp.bfloat16
HIGHEST = lax.Precision.HIGHEST

RMS_EPS = 1e-6
RW_HEAD = 64
RW_LNX_EPS = 64e-5
LRU_HEADS = 4
LRU_C = 8.0
N_EXPERTS = 16
CAPACITY_FACTOR = 2

LANES = 128
SUBLANES = 8
CHUNK = 64
LEAF = 16
VALUE_BISECT_STEPS = 48
VMEM_LIMIT_BYTES = 56 * 1024 * 1024


def _params(*semantics):
    return pltpu.CompilerParams(dimension_semantics=semantics, vmem_limit_bytes=VMEM_LIMIT_BYTES)


def _dot(a, b):
    return jnp.dot(a.astype(BF16), b.astype(BF16), preferred_element_type=F32)


def _dot_nt(a, b):
    return lax.dot_general(a.astype(BF16), b.astype(BF16), (((1,), (1,)), ((), ())),
                           preferred_element_type=F32)


def _dot_tn(a, b):
    return lax.dot_general(a.astype(BF16), b.astype(BF16), (((0,), (0,)), ((), ())),
                           preferred_element_type=F32)


def _dot_f32(a, b):
    return jnp.dot(a, b, preferred_element_type=F32, precision=HIGHEST)


def _rms(x, gain):
    return x * lax.rsqrt(jnp.mean(x * x, axis=-1, keepdims=True) + RMS_EPS) * gain


def _shift_prev(x, first_rows):
    n = x.shape[0]
    row = lax.broadcasted_iota(jnp.int32, x.shape, 0)
    out = []
    for k, edge in enumerate(first_rows, start=1):
        y = pltpu.roll(x, k, 0)
        for j in range(k):
            y = jnp.where(row == j, edge[j:j + 1], y)
        out.append(y)
    return out


def _shift_next(x, after_row):
    n = x.shape[0]
    row = lax.broadcasted_iota(jnp.int32, x.shape, 0)
    return jnp.where(row == n - 1, after_row, pltpu.roll(x, n - 1, 0))


def _halo_specs(ts, d):
    g = ts // SUBLANES

    def prev_map(b, i):
        return (b, jnp.maximum(i * g - 1, 0), 0, 0)

    def next_map(b, i, n_groups):
        return (b, jnp.minimum((i + 1) * g, n_groups - 1), 0, 0)

    return prev_map, next_map


def _rwkv_proj_kernel(x_ref, xp_ref, xn_ref, gain_ref, mu_ref, w0_ref, a0_ref,
                      wrkv_ref, w1_ref, w2_ref, a1_ref, a2_ref, g1_ref, g2_ref,
                      r_ref, k_ref, v_ref, a_fwd_ref, a_bwd_ref, lw_fwd_ref, lw_bwd_ref, g_ref):
    i = pl.program_id(1)
    last = pl.num_programs(1) - 1
    gain = gain_ref[...]
    h = _rms(x_ref[0], gain)
    h_before = jnp.where(i > 0, _rms(xp_ref[0, 0], gain)[SUBLANES - 1:SUBLANES], 0.0)
    h_after = jnp.where(i < last, _rms(xn_ref[0, 0], gain)[0:1], 0.0)
    (prev,) = _shift_prev(h, [h_before])
    nxt = _shift_next(h, h_after)
    xx = 0.5 * (prev + nxt) - h
    mu = mu_ref[...]

    def mix(p):
        return (h + xx * mu[p:p + 1]).astype(BF16)

    r_ref[0] = _dot(mix(0), wrkv_ref[0])
    k_ref[0] = _dot(mix(1), wrkv_ref[1])
    v_ref[0] = _dot(mix(2), wrkv_ref[2])

    tw = jnp.tanh(_dot(mix(3), w1_ref[...]))
    ta = _dot(mix(4), a1_ref[...])
    lane = lax.broadcasted_iota(jnp.int32, tw.shape, 1)
    for z, (lw_ref, a_ref) in enumerate(((lw_fwd_ref, a_fwd_ref), (lw_bwd_ref, a_bwd_ref))):
        mine = (lane // RW_HEAD) == z
        lora_w = _dot(jnp.where(mine, tw, 0.0), w2_ref[...])
        w_log = -jax.nn.softplus(-(w0_ref[z:z + 1] + lora_w)) - 0.5
        lw_ref[0] = -jnp.exp(w_log)
        lora_a = _dot(jnp.where(mine, ta, 0.0), a2_ref[...])
        a_ref[0] = jax.nn.sigmoid(a0_ref[z:z + 1] + lora_a)
    g_ref[0] = _dot(jax.nn.sigmoid(_dot(mix(5), g1_ref[...])), g2_ref[...])


def _rwkv_proj(x, gain, mu, w0, a0, wrkv, w1, w2, a1, a2, g1, g2, ts):
    b, s, d = x.shape
    nt = s // ts
    n_groups = s // SUBLANES
    x4 = x.reshape(b, n_groups, SUBLANES, d)
    prev_map, next_map = _halo_specs(ts, d)
    tile = pl.BlockSpec((1, ts, d), lambda bi, i: (bi, i, 0))
    halo = (1, 1, SUBLANES, d)

    def whole(a):
        return pl.BlockSpec(a.shape, lambda bi, i, nd=a.ndim: (0,) * nd)

    params = (gain, mu, w0, a0, wrkv, w1, w2, a1, a2, g1, g2)
    out = jax.ShapeDtypeStruct((b, s, d), F32)
    return pl.pallas_call(
        _rwkv_proj_kernel,
        grid=(b, nt),
        in_specs=[tile,
                  pl.BlockSpec(halo, prev_map),
                  pl.BlockSpec(halo, functools.partial(next_map, n_groups=n_groups))]
                 + [whole(p) for p in params],
        out_specs=[tile] * 8,
        out_shape=[out] * 8,
        compiler_params=_params("parallel", "parallel"),
        name="rwkv_proj",
    )(x, x4, x4, *params)


def _rwkv_scan_kernel(*refs, reverse, final, n_chunks):
    if final:
        (r_ref, k_ref, v_ref, a_ref, lw_ref, kk_scale_ref, ka_ref,
         yf_ref, a_other_ref, rk_ref, lng_ref, lnb_ref, y_ref, state_ref) = refs
    else:
        (r_ref, k_ref, v_ref, a_ref, lw_ref, kk_scale_ref, ka_ref, y_ref, state_ref) = refs

    @pl.when(pl.program_id(2) == 0)
    def _():
        state_ref[...] = jnp.zeros_like(state_ref)

    c, w2 = CHUNK, 2 * RW_HEAD
    row = lax.broadcasted_iota(jnp.int32, (c, w2), 0)
    lane = lax.broadcasted_iota(jnp.int32, (c, w2), 1)
    col = lane % RW_HEAD
    head0 = lane < RW_HEAD
    if reverse:
        strict, incl = col > row, col >= row
    else:
        strict, incl = col < row, col <= row
    leaf = (col // LEAF) == (row // LEAF)
    eye_pair = (col == row).astype(F32)
    rr = lax.broadcasted_iota(jnp.int32, (c, c), 0)
    cc = lax.broadcasted_iota(jnp.int32, (c, c), 1)
    cum_mat = ((cc >= rr) if reverse else (cc <= rr)).astype(F32)
    r2 = lax.broadcasted_iota(jnp.int32, (w2, w2), 0)
    c2 = lax.broadcasted_iota(jnp.int32, (w2, w2), 1)
    same_head = (r2 // RW_HEAD) == (c2 // RW_HEAD)
    head_ones = same_head.astype(F32)
    eye_w2 = r2 == c2

    def bd(m):
        return jnp.concatenate([jnp.where(head0, m, 0.0), jnp.where(head0, 0.0, m)], axis=0)

    def pm(a, m):
        return _dot(a, bd(m))

    kk_scale = kk_scale_ref[...]
    ka = ka_ref[...]
    order = range(n_chunks - 1, -1, -1) if reverse else range(n_chunks)
    for j in order:
        rows = pl.ds(j * c, c)
        r = r_ref[0, rows, :]
        k = k_ref[0, rows, :]
        v = v_ref[0, rows, :]
        a = a_ref[0, rows, :]
        lw = lw_ref[0, rows, :]
        kk = k * kk_scale
        kap = kk / jnp.maximum(jnp.sqrt(_dot_f32(kk * kk, head_ones)), 1e-12)
        kd = k * (1.0 + (a - 1.0) * ka)
        beta = a * kap
        cum = _dot_f32(cum_mat, lw)
        total = cum[0:1] if reverse else cum[c - 1:c]
        g_in = jnp.exp(cum)
        g_out = jnp.exp(-cum)
        g_tail = jnp.exp(total - cum)
        r_s = r * g_in
        kap_s = kap * jnp.exp(cum - lw)
        k_s = kd * g_out
        b_s = beta * g_out
        k_hat = kd * g_tail
        b_hat = beta * g_tail

        stacked = jnp.concatenate([jnp.where(head0, k_s, 0.0), jnp.where(head0, 0.0, k_s),
                                   jnp.where(head0, b_s, 0.0), jnp.where(head0, 0.0, b_s)], axis=0)
        gram = _dot_nt(jnp.concatenate([kap_s, r_s], axis=0), stacked)
        a_kk = jnp.where(strict, gram[:c, :w2], 0.0)
        n_mat = jnp.where(strict, gram[:c, w2:], 0.0)
        a_rk = jnp.where(incl, gram[c:, :w2], 0.0)
        a_rb = jnp.where(incl, gram[c:, w2:], 0.0)

        n_d = jnp.where(leaf, n_mat, 0.0)
        n_off = n_mat - n_d
        n2 = pm(n_d, n_d)
        n4 = pm(n2, n2)
        n8 = pm(n4, n4)
        t_d = pm(pm(pm(eye_pair - n_d, eye_pair + n2), eye_pair + n4), eye_pair + n8)
        e1 = pm(t_d, n_off)
        e2 = pm(e1, e1)
        t_mat = pm(pm(eye_pair - e1, eye_pair + e2), t_d)

        v_bd = bd(v)
        av = _dot(a_kk, v_bd)
        mw = _dot(t_mat, jnp.concatenate([bd(kap_s), bd(av)], axis=1))
        m_mat, w_mat = mw[:, :w2], mw[:, w2:]
        zeros_bd = jnp.zeros((w2, w2), F32)
        qy = _dot(jnp.concatenate([a_rk, a_rb], axis=1),
                  jnp.concatenate([jnp.concatenate([zeros_bd, v_bd], axis=1),
                                   jnp.concatenate([-bd(m_mat), -bd(w_mat)], axis=1)], axis=0))
        q_mat = r_s + qy[:, :w2]
        y_intra = qy[:, w2:]
        ph = _dot_tn(jnp.concatenate([k_hat, b_hat], axis=0),
                     jnp.concatenate([jnp.concatenate([jnp.zeros_like(v), v], axis=1),
                                      jnp.concatenate([-m_mat, -w_mat], axis=1)], axis=0))
        p_mat = jnp.where(same_head, ph[:, :w2], 0.0) + jnp.where(eye_w2, jnp.exp(total), 0.0)
        h_intra = jnp.where(same_head, ph[:, w2:], 0.0)

        state = state_ref[...]
        step = _dot(jnp.concatenate([q_mat, p_mat], axis=0), state)
        y = step[:c] + y_intra
        state_ref[...] = step[c:] + h_intra

        if final:
            y = y + yf_ref[0, rows, :]
            mean = _dot_f32(y, head_ones) * (1.0 / RW_HEAD)
            yc = y - mean
            var = _dot_f32(yc * yc, head_ones) * (1.0 / RW_HEAD)
            y = yc * lax.rsqrt(var + RW_LNX_EPS) * lng_ref[...] + lnb_ref[...]
            kd_other = k * (1.0 + (a_other_ref[0, rows, :] - 1.0) * ka)
            bonus = _dot_f32(r * (kd + kd_other) * rk_ref[...], head_ones)
            y = y + bonus * v
        y_ref[0, rows, :] = y


def _rwkv_scan(r, k, v, a, lw, kk_scale, ka, extra, *, reverse, ts):
    b, s, d = r.shape
    nt = s // ts
    w2 = 2 * RW_HEAD
    if reverse:
        tile = pl.BlockSpec((1, ts, w2), lambda bi, hp, i: (bi, nt - 1 - i, hp))
    else:
        tile = pl.BlockSpec((1, ts, w2), lambda bi, hp, i: (bi, i, hp))
    vec = pl.BlockSpec((1, w2), lambda bi, hp, i: (0, hp))
    final = extra is not None
    args = [r, k, v, a, lw, kk_scale, ka]
    specs = [tile] * 5 + [vec] * 2
    if final:
        yf, a_other, rk, lng, lnb = extra
        args += [yf, a_other, rk, lng, lnb]
        specs += [tile] * 2 + [vec] * 3
    return pl.pallas_call(
        functools.partial(_rwkv_scan_kernel, reverse=reverse, final=final, n_chunks=ts // CHUNK),
        grid=(b, d // w2, nt),
        in_specs=specs,
        out_specs=tile,
        out_shape=jax.ShapeDtypeStruct((b, s, d), F32),
        scratch_shapes=[pltpu.VMEM((w2, w2), F32)],
        compiler_params=_params("parallel", "parallel", "arbitrary"),
        name="rwkv_scan_bwd" if reverse else "rwkv_scan_fwd",
    )(*args)


def _gated_out_kernel(x_ref, y_ref, g_ref, w_ref, o_ref):
    o_ref[...] = x_ref[...] + _dot(y_ref[...] * g_ref[...], w_ref[...])


def _gated_out(x, y, g, w, tm):
    t, d = x.shape
    tile = pl.BlockSpec((tm, d), lambda i: (i, 0))
    return pl.pallas_call(
        _gated_out_kernel,
        grid=(t // tm,),
        in_specs=[tile, tile, tile, pl.BlockSpec(w.shape, lambda i: (0, 0))],
        out_specs=tile,
        out_shape=jax.ShapeDtypeStruct((t, d), F32),
        compiler_params=_params("parallel"),
        name="gated_out",
    )(x, y, g, w)


def _sconv_kernel(x_ref, xp_ref, xn_ref, gain_ref, win_ref, cw_ref, wout_ref, o_ref):
    i = pl.program_id(1)
    last = pl.num_programs(1) - 1
    d = x_ref.shape[-1]
    gain = gain_ref[...]
    x = x_ref[0]
    bcu = _dot(_rms(x, gain), win_ref[...])
    cu = bcu[:, d:2 * d] * bcu[:, 2 * d:]

    def edge_cu(ref):
        e = _dot(_rms(ref[0, 0], gain), win_ref[:, d:])
        return e[:, :d] * e[:, d:]

    cu_before = jnp.where(i > 0, edge_cu(xp_ref)[SUBLANES - 1:SUBLANES], 0.0)
    cu_after = jnp.where(i < last, edge_cu(xn_ref)[0:1], 0.0)
    (prev,) = _shift_prev(cu, [cu_before])
    nxt = _shift_next(cu, cu_after)
    cw = cw_ref[...]
    z = cw[0:1] * prev + cw[1:2] * cu + cw[2:3] * nxt
    o_ref[0] = x + _dot(bcu[:, :d] * z, wout_ref[...])


def _sconv_mixer(x, gain, w_in, conv_w, w_out, ts):
    b, s, d = x.shape
    n_groups = s // SUBLANES
    x4 = x.reshape(b, n_groups, SUBLANES, d)
    prev_map, next_map = _halo_specs(ts, d)
    tile = pl.BlockSpec((1, ts, d), lambda bi, i: (bi, i, 0))
    halo = (1, 1, SUBLANES, d)

    def whole(a):
        return pl.BlockSpec(a.shape, lambda bi, i, nd=a.ndim: (0,) * nd)

    params = (gain, w_in, conv_w, w_out)
    return pl.pallas_call(
        _sconv_kernel,
        grid=(b, s // ts),
        in_specs=[tile, pl.BlockSpec(halo, prev_map),
                  pl.BlockSpec(halo, functools.partial(next_map, n_groups=n_groups))]
                 + [whole(p) for p in params],
        out_specs=tile,
        out_shape=jax.ShapeDtypeStruct((b, s, d), F32),
        compiler_params=_params("parallel", "parallel"),
        name="sconv_mixer",
    )(x, x4, x4, *params)


def _tile_scan(a, b, reverse):
    n = a.shape[0]
    row = lax.broadcasted_iota(jnp.int32, a.shape, 0)
    s = 1
    while s < n:
        if reverse:
            a_far, b_far, valid = pltpu.roll(a, n - s, 0), pltpu.roll(b, n - s, 0), row < n - s
        else:
            a_far, b_far, valid = pltpu.roll(a, s, 0), pltpu.roll(b, s, 0), row >= s
        b = jnp.where(valid, a * b_far + b, b)
        a = jnp.where(valid, a * a_far, a)
        s *= 2
    return a, b


def _lru_gates(u, wa_ref, wx_ref, ba, bx, lam):
    d = u.shape[-1]
    blk = d // LRU_HEADS
    ub = u.astype(BF16)
    r_parts, i_parts = [], []
    for hh in range(LRU_HEADS):
        uh = ub[:, hh * blk:(hh + 1) * blk]
        r_parts.append(jnp.dot(uh, wa_ref[hh], preferred_element_type=F32))
        i_parts.append(jnp.dot(uh, wx_ref[hh], preferred_element_type=F32))
    r_gate = jax.nn.sigmoid(jnp.concatenate(r_parts, axis=1) + ba)
    i_gate = jax.nn.sigmoid(jnp.concatenate(i_parts, axis=1) + bx)
    log_a = -LRU_C * r_gate * jax.nn.softplus(-lam)
    th = jnp.tanh(log_a)
    return jnp.exp(log_a), jnp.sqrt(-2.0 * th / (1.0 - th)) * (i_gate * u)


def _lru_fwd_kernel(x_ref, xp_ref, xn_ref, gain_ref, win_ref, cw_ref, cb_ref,
                    wa_ref, wx_ref, ba_ref, bx_ref, lam_ref,
                    hf_ref, u_ref, gate_ref, carry_ref):
    i = pl.program_id(1)
    last = pl.num_programs(1) - 1
    d = x_ref.shape[-1]
    gain = gain_ref[...]

    @pl.when(i == 0)
    def _():
        carry_ref[...] = jnp.zeros_like(carry_ref)

    gu = _dot(_rms(x_ref[0], gain), win_ref[...])
    gate_ref[0] = jax.nn.gelu(gu[:, :d])
    u_raw = gu[:, d:]
    before = jnp.where(i > 0, _dot(_rms(xp_ref[0, 0], gain), win_ref[:, d:]), 0.0)
    after = jnp.where(i < last, _dot(_rms(xn_ref[0, 0], gain), win_ref[:, d:]), 0.0)
    prev1, prev2 = _shift_prev(u_raw, [before[SUBLANES - 1:], before[SUBLANES - 2:]])
    nxt = _shift_next(u_raw, after[0:1])
    cw = cw_ref[...]
    u = cw[0:1] * prev2 + cw[1:2] * prev1 + cw[2:3] * u_raw + cw[3:4] * nxt + cb_ref[...]
    u_ref[0] = u
    a, b = _lru_gates(u, wa_ref, wx_ref, ba_ref[...], bx_ref[...], lam_ref[...])
    a_run, h_loc = _tile_scan(a, b, reverse=False)
    hf = h_loc + a_run * carry_ref[...]
    hf_ref[0] = hf
    carry_ref[...] = hf[hf.shape[0] - 1:]


def _lru_bwd_kernel(x_ref, u_ref, gate_ref, hf_ref, wa_ref, wx_ref, ba_ref, bx_ref, lam_ref, wout_ref,
                    o_ref, carry_ref):
    @pl.when(pl.program_id(1) == 0)
    def _():
        carry_ref[...] = jnp.zeros_like(carry_ref)

    a, b = _lru_gates(u_ref[0], wa_ref, wx_ref, ba_ref[...], bx_ref[...], lam_ref[...])
    a_run, h_loc = _tile_scan(a, b, reverse=True)
    hb = h_loc + a_run * carry_ref[...]
    carry_ref[...] = hb[0:1]
    o_ref[0] = x_ref[0] + _dot((hf_ref[0] + hb) * gate_ref[0], wout_ref[...])


def _lru_mixer(x, gain, w_in, conv_w, conv_b, wa, wx, ba, bx, lam, w_out, ts):
    b, s, d = x.shape
    nt = s // ts
    n_groups = s // SUBLANES
    x4 = x.reshape(b, n_groups, SUBLANES, d)
    prev_map, next_map = _halo_specs(ts, d)
    tile = pl.BlockSpec((1, ts, d), lambda bi, i: (bi, i, 0))
    rtile = pl.BlockSpec((1, ts, d), lambda bi, i: (bi, nt - 1 - i, 0))
    halo = (1, 1, SUBLANES, d)

    def whole(a):
        return pl.BlockSpec(a.shape, lambda bi, i, nd=a.ndim: (0,) * nd)

    out = jax.ShapeDtypeStruct((b, s, d), F32)
    fwd_params = (gain, w_in, conv_w, conv_b, wa[0], wx[0], ba[0:1], bx[0:1], lam[0:1])
    hf, u, gate = pl.pallas_call(
        _lru_fwd_kernel,
        grid=(b, nt),
        in_specs=[tile, pl.BlockSpec(halo, prev_map),
                  pl.BlockSpec(halo, functools.partial(next_map, n_groups=n_groups))]
                 + [whole(p) for p in fwd_params],
        out_specs=[tile] * 3,
        out_shape=[out] * 3,
        scratch_shapes=[pltpu.VMEM((1, d), F32)],
        compiler_params=_params("parallel", "arbitrary"),
        name="lru_fwd",
    )(x, x4, x4, *fwd_params)
    bwd_params = (wa[1], wx[1], ba[1:2], bx[1:2], lam[1:2], w_out)
    return pl.pallas_call(
        _lru_bwd_kernel,
        grid=(b, nt),
        in_specs=[rtile] * 4 + [whole(p) for p in bwd_params],
        out_specs=rtile,
        out_shape=out,
        scratch_shapes=[pltpu.VMEM((1, d), F32)],
        compiler_params=_params("parallel", "arbitrary"),
        name="lru_bwd",
    )(x, u, gate, hf, *bwd_params)


def _route_kernel(x_ref, gain_ref, rw_ref, xin_ref, gcol_ref, rank_ref, h_ref, rank_scr, gate_scr, *, cap):
    e = pl.program_id(1)
    n_exp, s = rank_scr.shape

    @pl.when(e == 0)
    def _():
        h = _rms(x_ref[0], gain_ref[...])
        h_ref[...] = h.astype(BF16)
        logits = lax.dot_general(rw_ref[...], h, (((1,), (1,)), ((), ())),
                                 preferred_element_type=F32, precision=HIGHEST)
        ex = jnp.exp(logits - jnp.max(logits, axis=0, keepdims=True))
        aff = ex / jnp.sum(ex, axis=0, keepdims=True)
        def count(pred):
            return jnp.sum(jnp.where(pred, 1.0, 0.0), axis=1, keepdims=True)

        def binade_step(_, jj):
            j_fail, j_ok = jj
            j_mid = jnp.floor(0.5 * (j_fail + j_ok))
            ok = count(aff >= jnp.exp2(1.0 - j_mid)) >= cap
            return jnp.where(ok, j_fail, j_mid), jnp.where(ok, j_mid, j_ok)

        _, j_ok = lax.fori_loop(0, 8, binade_step,
                                (jnp.full((n_exp, 1), -1.0, F32), jnp.full((n_exp, 1), 255.0, F32)))

        def value_step(_, lohi):
            lo, hi = lohi
            mid = 0.5 * (lo + hi)
            ok = count(aff >= mid) >= cap
            return jnp.where(ok, mid, lo), jnp.where(ok, hi, mid)

        lo, hi = lax.fori_loop(0, VALUE_BISECT_STEPS, value_step,
                               (jnp.exp2(1.0 - j_ok), jnp.exp2(2.0 - j_ok)))
        above = aff >= hi
        tied = (aff >= lo) & jnp.logical_not(above)
        need = cap - count(above)
        idx = lax.broadcasted_iota(jnp.int32, (n_exp, s), 1).astype(F32)

        def index_step(_, mm):
            m_fail, m_ok = mm
            m_mid = jnp.floor(0.5 * (m_fail + m_ok))
            ok = count(tied & (idx <= m_mid)) >= need
            return jnp.where(ok, m_fail, m_mid), jnp.where(ok, m_mid, m_ok)

        _, m_ok = lax.fori_loop(0, (s - 1).bit_length(), index_step,
                                (jnp.full((n_exp, 1), -1.0, F32), jnp.full((n_exp, 1), s - 1.0, F32)))
        chosen = above | (tied & (idx <= m_ok))
        gate_scr[...] = jnp.where(chosen, aff, 0.0)
        sel = chosen.astype(F32)
        ri = lax.broadcasted_iota(jnp.int32, (LANES, LANES), 0)
        ci = lax.broadcasted_iota(jnp.int32, (LANES, LANES), 1)
        prefix = (ri <= ci).astype(BF16)
        seen = jnp.zeros((n_exp, 1), F32)
        for j in range(s // LANES):
            sj = sel[:, j * LANES:(j + 1) * LANES]
            pos = jnp.dot(sj.astype(BF16), prefix, preferred_element_type=F32) + seen
            rank_scr[:, j * LANES:(j + 1) * LANES] = jnp.where(sj > 0.0, pos, 0.0)
            seen = seen + jnp.sum(sj, axis=1, keepdims=True)

    rank_row = rank_scr[pl.ds(e, 1), :]
    slot = (lax.broadcasted_iota(jnp.int32, (cap, s), 0) + 1).astype(F32)
    hit = rank_row == slot
    xin_ref[0, 0] = jnp.dot(jnp.where(hit, 1.0, 0.0).astype(BF16), h_ref[...],
                            preferred_element_type=F32).astype(BF16)
    gcol_ref[0, 0] = jnp.sum(jnp.where(hit, gate_scr[pl.ds(e, 1), :], 0.0), axis=1, keepdims=True)
    rank_ref[0, 0] = rank_row


def _route(x, gain, router_wt, cap):
    b, s, d = x.shape
    n_exp = router_wt.shape[0]
    return pl.pallas_call(
        functools.partial(_route_kernel, cap=cap),
        grid=(b, n_exp),
        in_specs=[pl.BlockSpec((1, s, d), lambda bi, e: (bi, 0, 0)),
                  pl.BlockSpec((1, d), lambda bi, e: (0, 0)),
                  pl.BlockSpec((n_exp, d), lambda bi, e: (0, 0))],
        out_specs=[pl.BlockSpec((1, 1, cap, d), lambda bi, e: (bi, e, 0, 0)),
                   pl.BlockSpec((1, 1, cap, 1), lambda bi, e: (bi, e, 0, 0)),
                   pl.BlockSpec((1, 1, 1, s), lambda bi, e: (bi, e, 0, 0))],
        out_shape=[jax.ShapeDtypeStruct((b, n_exp, cap, d), BF16),
                   jax.ShapeDtypeStruct((b, n_exp, cap, 1), F32),
                   jax.ShapeDtypeStruct((b, n_exp, 1, s), F32)],
        scratch_shapes=[pltpu.VMEM((s, d), BF16), pltpu.VMEM((n_exp, s), F32), pltpu.VMEM((n_exp, s), F32)],
        compiler_params=_params("parallel", "arbitrary"),
        name="moe_route",
    )(x, gain, router_wt)


def _ffn_kernel(xin_ref, gcol_ref, wg_ref, wu_ref, wd_ref, o_ref, acc_ref):
    f = pl.program_id(1)
    b, _, cap, d = xin_ref.shape
    xin = xin_ref[...].reshape(b * cap, d)
    hid = jax.nn.silu(_dot(xin, wg_ref[0])) * _dot(xin, wu_ref[0])
    part = _dot(hid, wd_ref[0])

    @pl.when(f == 0)
    def _():
        acc_ref[...] = part

    @pl.when(f > 0)
    def _():
        acc_ref[...] += part

    @pl.when(f == pl.num_programs(1) - 1)
    def _():
        o_ref[...] = (acc_ref[...] * gcol_ref[...].reshape(b * cap, 1)).astype(BF16).reshape(o_ref.shape)


def _ffn(xin, gcol, w_gate, w_up, w_down, tf):
    b, n_exp, cap, d = xin.shape
    ff = w_gate.shape[-1]
    tok = pl.BlockSpec((b, 1, cap, d), lambda e, f: (0, e, 0, 0))
    return pl.pallas_call(
        _ffn_kernel,
        grid=(n_exp, ff // tf),
        in_specs=[tok,
                  pl.BlockSpec((b, 1, cap, 1), lambda e, f: (0, e, 0, 0)),
                  pl.BlockSpec((1, d, tf), lambda e, f: (e, 0, f)),
                  pl.BlockSpec((1, d, tf), lambda e, f: (e, 0, f)),
                  pl.BlockSpec((1, tf, d), lambda e, f: (e, f, 0))],
        out_specs=tok,
        out_shape=jax.ShapeDtypeStruct(xin.shape, BF16),
        scratch_shapes=[pltpu.VMEM((b * cap, d), F32)],
        compiler_params=_params("parallel", "arbitrary"),
        name="moe_ffn",
    )(xin, gcol, w_gate, w_up, w_down)


def _combine_kernel(x_ref, rank_ref, y_ref, gain_ref, o_ref, *, final_norm):
    e = pl.program_id(2)
    cap = y_ref.shape[2]
    ts = o_ref.shape[1]

    @pl.when(e == 0)
    def _():
        o_ref[...] = x_ref[...]

    slot = (lax.broadcasted_iota(jnp.int32, (cap, ts), 0) + 1).astype(F32)
    onehot = jnp.where(rank_ref[0, 0] == slot, 1.0, 0.0)
    o_ref[0] += _dot_tn(onehot, y_ref[0, 0])

    if final_norm:
        @pl.when(e == pl.num_programs(2) - 1)
        def _():
            o_ref[0] = _rms(o_ref[0], gain_ref[...])


def _combine(x, rank, yexp, gain, ts, final_norm):
    b, s, d = x.shape
    n_exp, cap = yexp.shape[1], yexp.shape[2]
    tile = pl.BlockSpec((1, ts, d), lambda bi, i, e: (bi, i, 0))
    return pl.pallas_call(
        functools.partial(_combine_kernel, final_norm=final_norm),
        grid=(b, s // ts, n_exp),
        in_specs=[tile,
                  pl.BlockSpec((1, 1, 1, ts), lambda bi, i, e: (bi, e, 0, i)),
                  pl.BlockSpec((1, 1, cap, d), lambda bi, i, e: (bi, e, 0, 0)),
                  pl.BlockSpec((1, d), lambda bi, i, e: (0, 0))],
        out_specs=tile,
        out_shape=jax.ShapeDtypeStruct((b, s, d), F32),
        compiler_params=_params("parallel", "parallel", "arbitrary"),
        name="moe_combine",
    )(x, rank, yexp, gain)


def _moe(x, gain, router_w, w_gate, w_up, w_down, final_gain):
    b, s, d = x.shape
    cap = CAPACITY_FACTOR * s // N_EXPERTS
    xin, gcol, rank = _route(x, gain, router_w.T, cap)
    yexp = _ffn(xin, gcol, w_gate, w_up, w_down, tf=512)
    final_norm = final_gain is not None
    return _combine(x, rank, yexp, final_gain if final_norm else gain, ts=s // 2, final_norm=final_norm)


def _rwkv_layer(x, gain, mu, w_in, w0, w1, w2, a0, a1, a2, g1, g2, k_k, k_a, r_k, lnx_g, lnx_b, w_out):
    b, s, d = x.shape
    row = lambda t: t.reshape(1, d)
    mu8 = jnp.concatenate([mu, jnp.zeros((SUBLANES - mu.shape[0], d), F32)], axis=0)
    cat_in = lambda t: jnp.concatenate([t[0], t[1]], axis=1).astype(BF16)
    cat_out = lambda t: jnp.concatenate([t[0], t[1]], axis=0).astype(BF16)
    r, k, v, a_f, a_b, lw_f, lw_b, g = _rwkv_proj(
        x, row(gain), mu8, w0, a0, w_in.astype(BF16), cat_in(w1), cat_out(w2), cat_in(a1), cat_out(a2),
        g1.astype(BF16), g2.astype(BF16), ts=256)
    y_f = _rwkv_scan(r, k, v, a_f, lw_f, row(k_k), row(k_a), None, reverse=False, ts=256)
    y = _rwkv_scan(r, k, v, a_b, lw_b, row(k_k), row(k_a),
                   (y_f, a_f, row(r_k), row(lnx_g), row(lnx_b)), reverse=True, ts=256)
    t = b * s
    out = _gated_out(x.reshape(t, d), y.reshape(t, d), g.reshape(t, d), w_out.astype(BF16), tm=512)
    return out.reshape(b, s, d)


def kernel(x, norm_mix_g, norm_ffn_g, router_w, exp_w_gate, exp_w_up, exp_w_down, final_norm_g, rw_mu, rw_w_in, rw_w0, rw_w1, rw_w2, rw_a0, rw_a1, rw_a2, rw_g1, rw_g2, rw_k_k, rw_k_a, rw_r_k, rw_lnx_g, rw_lnx_b, rw_w_out, sc_w_in, sc_conv_w, sc_w_out, lru_w_in, lru_conv_w, lru_conv_b, lru_w_a, lru_b_a, lru_w_x, lru_b_x, lru_lambda, lru_w_out):
    depth, d = norm_mix_g.shape
    ia = ib = ic = 0
    for i in range(depth):
        gain = norm_mix_g[i].reshape(1, d)
        m = i % 3
        if m == 0:
            x = _rwkv_layer(x, norm_mix_g[i], rw_mu[ia], rw_w_in[ia], rw_w0[ia], rw_w1[ia], rw_w2[ia],
                            rw_a0[ia], rw_a1[ia], rw_a2[ia], rw_g1[ia], rw_g2[ia], rw_k_k[ia], rw_k_a[ia],
                            rw_r_k[ia], rw_lnx_g[ia], rw_lnx_b[ia], rw_w_out[ia])
            ia += 1
        elif m == 1:
            cw = jnp.concatenate([sc_conv_w[ib], jnp.zeros((SUBLANES - sc_conv_w.shape[1], d), F32)], axis=0)
            x = _sconv_mixer(x, gain, sc_w_in[ib].astype(BF16), cw, sc_w_out[ib].astype(BF16), ts=256)
            ib += 1
        else:
            cw = jnp.concatenate([lru_conv_w[ic], jnp.zeros((SUBLANES - lru_conv_w.shape[1], d), F32)], axis=0)
            x = _lru_mixer(x, gain, lru_w_in[ic].astype(BF16), cw, lru_conv_b[ic].reshape(1, d),
                           lru_w_a[ic].astype(BF16), lru_w_x[ic].astype(BF16), lru_b_a[ic], lru_b_x[ic],
                           lru_lambda[ic], lru_w_out[ic].astype(BF16), ts=256)
            ic += 1
        x = _moe(x, norm_ffn_g[i].reshape(1, d), router_w[i], exp_w_gate[i], exp_w_up[i], exp_w_down[i],
                 final_norm_g.reshape(1, d) if i == depth - 1 else None)
    return x
```

```python
import functools

import jax
import jax.numpy as jnp
from jax import lax
from jax.experimental import pallas as pl
from jax.experimental.pallas import tpu as pltpu

F32 = jnp.float32
BF16 = jnp.bfloat16
HIGHEST = lax.Precision.HIGHEST

RMS_EPS = 1e-6
RW_HEAD = 64
RW_LNX_EPS = 64e-5
LRU_HEADS = 4
LRU_C = 8.0
N_EXPERTS = 16
CAPACITY_FACTOR = 2

LANES = 128
SUBLANES = 8
CHUNK = 64
LEAF = 16
RW_SCAN_TILE = 512
VALUE_BISECT_STEPS = 48
VMEM_LIMIT_BYTES = 56 * 1024 * 1024


def _params(*semantics):
    return pltpu.CompilerParams(dimension_semantics=semantics, vmem_limit_bytes=VMEM_LIMIT_BYTES)


def _dot(a, b):
    return jnp.dot(a.astype(BF16), b.astype(BF16), preferred_element_type=F32)


def _dot_nt(a, b):
    return lax.dot_general(a.astype(BF16), b.astype(BF16), (((1,), (1,)), ((), ())),
                           preferred_element_type=F32)


def _dot_tn(a, b):
    return lax.dot_general(a.astype(BF16), b.astype(BF16), (((0,), (0,)), ((), ())),
                           preferred_element_type=F32)


def _dot_f32(a, b):
    return jnp.dot(a, b, preferred_element_type=F32, precision=HIGHEST)


def _rms(x, gain):
    return x * lax.rsqrt(jnp.mean(x * x, axis=-1, keepdims=True) + RMS_EPS) * gain


def _shift_prev(x, first_rows):
    n = x.shape[0]
    row = lax.broadcasted_iota(jnp.int32, x.shape, 0)
    out = []
    for k, edge in enumerate(first_rows, start=1):
        y = pltpu.roll(x, k, 0)
        for j in range(k):
            y = jnp.where(row == j, edge[j:j + 1], y)
        out.append(y)
    return out


def _shift_next(x, after_row):
    n = x.shape[0]
    row = lax.broadcasted_iota(jnp.int32, x.shape, 0)
    return jnp.where(row == n - 1, after_row, pltpu.roll(x, n - 1, 0))


def _halo_specs(ts, d):
    g = ts // SUBLANES

    def prev_map(b, i):
        return (b, jnp.maximum(i * g - 1, 0), 0, 0)

    def next_map(b, i, n_groups):
        return (b, jnp.minimum((i + 1) * g, n_groups - 1), 0, 0)

    return prev_map, next_map


def _rwkv_proj_kernel(x_ref, xp_ref, xn_ref, gain_ref, mu_ref, w0_ref, a0_ref,
                      wrkv_ref, w1_ref, w2_ref, a1_ref, a2_ref, g1_ref, g2_ref,
                      r_ref, k_ref, v_ref, a_fwd_ref, a_bwd_ref, lw_fwd_ref, lw_bwd_ref, g_ref):
    i = pl.program_id(1)
    last = pl.num_programs(1) - 1
    gain = gain_ref[...]
    h = _rms(x_ref[0], gain)
    h_before = jnp.where(i > 0, _rms(xp_ref[0, 0], gain)[SUBLANES - 1:SUBLANES], 0.0)
    h_after = jnp.where(i < last, _rms(xn_ref[0, 0], gain)[0:1], 0.0)
    (prev,) = _shift_prev(h, [h_before])
    nxt = _shift_next(h, h_after)
    xx = 0.5 * (prev + nxt) - h
    mu = mu_ref[...]

    def mix(p):
        return (h + xx * mu[p:p + 1]).astype(BF16)

    r_ref[0] = _dot(mix(0), wrkv_ref[0])
    k_ref[0] = _dot(mix(1), wrkv_ref[1])
    v_ref[0] = _dot(mix(2), wrkv_ref[2])

    tw = jnp.tanh(_dot(mix(3), w1_ref[...]))
    ta = _dot(mix(4), a1_ref[...])
    lane = lax.broadcasted_iota(jnp.int32, tw.shape, 1)
    for z, (lw_ref, a_ref) in enumerate(((lw_fwd_ref, a_fwd_ref), (lw_bwd_ref, a_bwd_ref))):
        mine = (lane // RW_HEAD) == z
        lora_w = _dot(jnp.where(mine, tw, 0.0), w2_ref[...])
        w_log = -jax.nn.softplus(-(w0_ref[z:z + 1] + lora_w)) - 0.5
        lw_ref[0] = -jnp.exp(w_log)
        lora_a = _dot(jnp.where(mine, ta, 0.0), a2_ref[...])
        a_ref[0] = jax.nn.sigmoid(a0_ref[z:z + 1] + lora_a)
    g_ref[0] = _dot(jax.nn.sigmoid(_dot(mix(5), g1_ref[...])), g2_ref[...])


def _rwkv_proj(x, gain, mu, w0, a0, wrkv, w1, w2, a1, a2, g1, g2, ts):
    b, s, d = x.shape
    nt = s // ts
    n_groups = s // SUBLANES
    x4 = x.reshape(b, n_groups, SUBLANES, d)
    prev_map, next_map = _halo_specs(ts, d)
    tile = pl.BlockSpec((1, ts, d), lambda bi, i: (bi, i, 0))
    halo = (1, 1, SUBLANES, d)

    def whole(a):
        return pl.BlockSpec(a.shape, lambda bi, i, nd=a.ndim: (0,) * nd)

    params = (gain, mu, w0, a0, wrkv, w1, w2, a1, a2, g1, g2)
    out = jax.ShapeDtypeStruct((b, s, d), F32)
    return pl.pallas_call(
        _rwkv_proj_kernel,
        grid=(b, nt),
        in_specs=[tile,
                  pl.BlockSpec(halo, prev_map),
                  pl.BlockSpec(halo, functools.partial(next_map, n_groups=n_groups))]
                 + [whole(p) for p in params],
        out_specs=[tile] * 8,
        out_shape=[out] * 8,
        compiler_params=_params("parallel", "parallel"),
        name="rwkv_proj",
    )(x, x4, x4, *params)


def _rwkv_scan_kernel(*refs, reverse, final, n_chunks):
    if final:
        (r_ref, k_ref, v_ref, a_ref, lw_ref, kk_scale_ref, ka_ref,
         yf_ref, a_other_ref, rk_ref, lng_ref, lnb_ref, y_ref, state_ref) = refs
    else:
        (r_ref, k_ref, v_ref, a_ref, lw_ref, kk_scale_ref, ka_ref, y_ref, state_ref) = refs

    @pl.when(pl.program_id(2) == 0)
    def _():
        state_ref[...] = jnp.zeros_like(state_ref)

    c, w2 = CHUNK, 2 * RW_HEAD
    row = lax.broadcasted_iota(jnp.int32, (c, w2), 0)
    lane = lax.broadcasted_iota(jnp.int32, (c, w2), 1)
    col = lane % RW_HEAD
    head0 = lane < RW_HEAD
    if reverse:
        strict, incl = col > row, col >= row
    else:
        strict, incl = col < row, col <= row
    leaf = (col // LEAF) == (row // LEAF)
    eye_pair = (col == row).astype(F32)
    rr = lax.broadcasted_iota(jnp.int32, (c, c), 0)
    cc = lax.broadcasted_iota(jnp.int32, (c, c), 1)
    cum_mat = ((cc >= rr) if reverse else (cc <= rr)).astype(F32)
    r2 = lax.broadcasted_iota(jnp.int32, (w2, w2), 0)
    c2 = lax.broadcasted_iota(jnp.int32, (w2, w2), 1)
    same_head = (r2 // RW_HEAD) == (c2 // RW_HEAD)
    head_ones = same_head.astype(F32)
    eye_w2 = r2 == c2

    def bd(m):
        return jnp.concatenate([jnp.where(head0, m, 0.0), jnp.where(head0, 0.0, m)], axis=0)

    def pm(a, m):
        return _dot(a, bd(m))

    kk_scale = kk_scale_ref[...]
    ka = ka_ref[...]
    order = list(range(n_chunks - 1, -1, -1) if reverse else range(n_chunks))

    def each(fn, *lists):
        return [fn(*args) for args in zip(*lists)]

    rows = [pl.ds(j * c, c) for j in order]
    r = [r_ref[0, rw, :] for rw in rows]
    k = [k_ref[0, rw, :] for rw in rows]
    v = [v_ref[0, rw, :] for rw in rows]
    a = [a_ref[0, rw, :] for rw in rows]
    lw = [lw_ref[0, rw, :] for rw in rows]
    kk = each(lambda x: x * kk_scale, k)
    sumsq = each(lambda x: _dot_f32(x * x, head_ones), kk)
    cum = each(lambda x: _dot_f32(cum_mat, x), lw)
    kap = each(lambda x, ss: x / jnp.maximum(jnp.sqrt(ss), 1e-12), kk, sumsq)
    kd = each(lambda x, aa: x * (1.0 + (aa - 1.0) * ka), k, a)
    beta = each(lambda aa, kp: aa * kp, a, kap)
    total = each(lambda x: x[0:1] if reverse else x[c - 1:c], cum)
    g_out = each(lambda x: jnp.exp(-x), cum)
    g_tail = each(lambda t, x: jnp.exp(t - x), total, cum)
    r_s = each(lambda x, cm: x * jnp.exp(cm), r, cum)
    kap_s = each(lambda x, cm, l: x * jnp.exp(cm - l), kap, cum, lw)
    k_s = each(lambda x, g: x * g, kd, g_out)
    b_s = each(lambda x, g: x * g, beta, g_out)
    k_hat = each(lambda x, g: x * g, kd, g_tail)
    b_hat = each(lambda x, g: x * g, beta, g_tail)

    def gram_fn(kp, rs, ks, bs):
        stacked = jnp.concatenate([jnp.where(head0, ks, 0.0), jnp.where(head0, 0.0, ks),
                                   jnp.where(head0, bs, 0.0), jnp.where(head0, 0.0, bs)], axis=0)
        return _dot_nt(jnp.concatenate([kp, rs], axis=0), stacked)

    gram = each(gram_fn, kap_s, r_s, k_s, b_s)
    a_kk = each(lambda g: jnp.where(strict, g[:c, :w2], 0.0), gram)
    n_mat = each(lambda g: jnp.where(strict, g[:c, w2:], 0.0), gram)
    a_rk = each(lambda g: jnp.where(incl, g[c:, :w2], 0.0), gram)
    a_rb = each(lambda g: jnp.where(incl, g[c:, w2:], 0.0), gram)

    n_d = each(lambda n: jnp.where(leaf, n, 0.0), n_mat)
    n_off = each(lambda n, nd: n - nd, n_mat, n_d)
    n2 = each(pm, n_d, n_d)
    n4 = each(pm, n2, n2)
    n8 = each(pm, n4, n4)
    t_d = each(lambda x, y: pm(eye_pair - x, eye_pair + y), n_d, n2)
    t_d = each(lambda x, y: pm(x, eye_pair + y), t_d, n4)
    t_d = each(lambda x, y: pm(x, eye_pair + y), t_d, n8)
    e1 = each(pm, t_d, n_off)
    e2 = each(pm, e1, e1)
    t_mat = each(lambda x, y: pm(eye_pair - x, eye_pair + y), e1, e2)
    t_mat = each(pm, t_mat, t_d)

    v_bd = each(bd, v)
    av = each(_dot, a_kk, v_bd)
    mw = each(lambda t, kp, x: _dot(t, jnp.concatenate([bd(kp), bd(x)], axis=1)), t_mat, kap_s, av)
    m_mat = each(lambda x: x[:, :w2], mw)
    w_mat = each(lambda x: x[:, w2:], mw)
    zeros_bd = jnp.zeros((w2, w2), F32)

    def qy_fn(ark, arb, vb, m, w):
        return _dot(jnp.concatenate([ark, arb], axis=1),
                    jnp.concatenate([jnp.concatenate([zeros_bd, vb], axis=1),
                                     jnp.concatenate([-bd(m), -bd(w)], axis=1)], axis=0))

    def ph_fn(kh, bh, vv, m, w):
        return _dot_tn(jnp.concatenate([kh, bh], axis=0),
                       jnp.concatenate([jnp.concatenate([jnp.zeros_like(vv), vv], axis=1),
                                        jnp.concatenate([-m, -w], axis=1)], axis=0))

    qy = each(qy_fn, a_rk, a_rb, v_bd, m_mat, w_mat)
    ph = each(ph_fn, k_hat, b_hat, v, m_mat, w_mat)
    q_mat = each(lambda rs, x: rs + x[:, :w2], r_s, qy)
    y_intra = each(lambda x: x[:, w2:], qy)
    p_mat = each(lambda x, t: jnp.where(same_head, x[:, :w2], 0.0) + jnp.where(eye_w2, jnp.exp(t), 0.0),
                 ph, total)
    h_intra = each(lambda x: jnp.where(same_head, x[:, w2:], 0.0), ph)

    state = state_ref[...]
    y = []
    for q, p, yi, hi in zip(q_mat, p_mat, y_intra, h_intra):
        step = _dot(jnp.concatenate([q, p], axis=0), state)
        y.append(step[:c] + yi)
        state = step[c:] + hi
    state_ref[...] = state

    if final:
        y = each(lambda x, rw: x + yf_ref[0, rw, :], y, rows)
        mean = each(lambda x: _dot_f32(x, head_ones) * (1.0 / RW_HEAD), y)
        yc = each(lambda x, m: x - m, y, mean)
        var = each(lambda x: _dot_f32(x * x, head_ones) * (1.0 / RW_HEAD), yc)
        kd_other = each(lambda x, rw: x * (1.0 + (a_other_ref[0, rw, :] - 1.0) * ka), k, rows)
        bonus = each(lambda rr_, x, xo: _dot_f32(rr_ * (x + xo) * rk_ref[...], head_ones), r, kd, kd_other)
        y = each(lambda x, vr, bo, vv: x * lax.rsqrt(vr + RW_LNX_EPS) * lng_ref[...] + lnb_ref[...] + bo * vv,
                 yc, var, bonus, v)
    for rw, x in zip(rows, y):
        y_ref[0, rw, :] = x


def _rwkv_scan(r, k, v, a, lw, kk_scale, ka, extra, *, reverse, ts):
    b, s, d = r.shape
    nt = s // ts
    w2 = 2 * RW_HEAD
    if reverse:
        tile = pl.BlockSpec((1, ts, w2), lambda bi, hp, i: (bi, nt - 1 - i, hp))
    else:
        tile = pl.BlockSpec((1, ts, w2), lambda bi, hp, i: (bi, i, hp))
    vec = pl.BlockSpec((1, w2), lambda bi, hp, i: (0, hp))
    final = extra is not None
    args = [r, k, v, a, lw, kk_scale, ka]
    specs = [tile] * 5 + [vec] * 2
    if final:
        yf, a_other, rk, lng, lnb = extra
        args += [yf, a_other, rk, lng, lnb]
        specs += [tile] * 2 + [vec] * 3
    return pl.pallas_call(
        functools.partial(_rwkv_scan_kernel, reverse=reverse, final=final, n_chunks=ts // CHUNK),
        grid=(b, d // w2, nt),
        in_specs=specs,
        out_specs=tile,
        out_shape=jax.ShapeDtypeStruct((b, s, d), F32),
        scratch_shapes=[pltpu.VMEM((w2, w2), F32)],
        compiler_params=_params("parallel", "parallel", "arbitrary"),
        name="rwkv_scan_bwd" if reverse else "rwkv_scan_fwd",
    )(*args)


def _gated_out_kernel(x_ref, y_ref, g_ref, w_ref, o_ref):
    o_ref[...] = x_ref[...] + _dot(y_ref[...] * g_ref[...], w_ref[...])


def _gated_out(x, y, g, w, tm):
    t, d = x.shape
    tile = pl.BlockSpec((tm, d), lambda i: (i, 0))
    return pl.pallas_call(
        _gated_out_kernel,
        grid=(t // tm,),
        in_specs=[tile, tile, tile, pl.BlockSpec(w.shape, lambda i: (0, 0))],
        out_specs=tile,
        out_shape=jax.ShapeDtypeStruct((t, d), F32),
        compiler_params=_params("parallel"),
        name="gated_out",
    )(x, y, g, w)


def _sconv_kernel(x_ref, xp_ref, xn_ref, gain_ref, win_ref, cw_ref, wout_ref, o_ref):
    i = pl.program_id(1)
    last = pl.num_programs(1) - 1
    d = x_ref.shape[-1]
    gain = gain_ref[...]
    x = x_ref[0]
    bcu = _dot(_rms(x, gain), win_ref[...])
    cu = bcu[:, d:2 * d] * bcu[:, 2 * d:]

    def edge_cu(ref):
        e = _dot(_rms(ref[0, 0], gain), win_ref[:, d:])
        return e[:, :d] * e[:, d:]

    cu_before = jnp.where(i > 0, edge_cu(xp_ref)[SUBLANES - 1:SUBLANES], 0.0)
    cu_after = jnp.where(i < last, edge_cu(xn_ref)[0:1], 0.0)
    (prev,) = _shift_prev(cu, [cu_before])
    nxt = _shift_next(cu, cu_after)
    cw = cw_ref[...]
    z = cw[0:1] * prev + cw[1:2] * cu + cw[2:3] * nxt
    o_ref[0] = x + _dot(bcu[:, :d] * z, wout_ref[...])


def _sconv_mixer(x, gain, w_in, conv_w, w_out, ts):
    b, s, d = x.shape
    n_groups = s // SUBLANES
    x4 = x.reshape(b, n_groups, SUBLANES, d)
    prev_map, next_map = _halo_specs(ts, d)
    tile = pl.BlockSpec((1, ts, d), lambda bi, i: (bi, i, 0))
    halo = (1, 1, SUBLANES, d)

    def whole(a):
        return pl.BlockSpec(a.shape, lambda bi, i, nd=a.ndim: (0,) * nd)

    params = (gain, w_in, conv_w, w_out)
    return pl.pallas_call(
        _sconv_kernel,
        grid=(b, s // ts),
        in_specs=[tile, pl.BlockSpec(halo, prev_map),
                  pl.BlockSpec(halo, functools.partial(next_map, n_groups=n_groups))]
                 + [whole(p) for p in params],
        out_specs=tile,
        out_shape=jax.ShapeDtypeStruct((b, s, d), F32),
        compiler_params=_params("parallel", "parallel"),
        name="sconv_mixer",
    )(x, x4, x4, *params)


def _tile_scan(a, b, reverse):
    n = a.shape[0]
    row = lax.broadcasted_iota(jnp.int32, a.shape, 0)
    s = 1
    while s < n:
        if reverse:
            a_far, b_far, valid = pltpu.roll(a, n - s, 0), pltpu.roll(b, n - s, 0), row < n - s
        else:
            a_far, b_far, valid = pltpu.roll(a, s, 0), pltpu.roll(b, s, 0), row >= s
        b = jnp.where(valid, a * b_far + b, b)
        a = jnp.where(valid, a * a_far, a)
        s *= 2
    return a, b


def _lru_gates(u, wa_ref, wx_ref, ba, bx, lam):
    d = u.shape[-1]
    blk = d // LRU_HEADS
    ub = u.astype(BF16)
    r_parts, i_parts = [], []
    for hh in range(LRU_HEADS):
        uh = ub[:, hh * blk:(hh + 1) * blk]
        r_parts.append(jnp.dot(uh, wa_ref[hh], preferred_element_type=F32))
        i_parts.append(jnp.dot(uh, wx_ref[hh], preferred_element_type=F32))
    r_gate = jax.nn.sigmoid(jnp.concatenate(r_parts, axis=1) + ba)
    i_gate = jax.nn.sigmoid(jnp.concatenate(i_parts, axis=1) + bx)
    log_a = -LRU_C * r_gate * jax.nn.softplus(-lam)
    th = jnp.tanh(log_a)
    return jnp.exp(log_a), jnp.sqrt(-2.0 * th / (1.0 - th)) * (i_gate * u)


def _lru_fwd_kernel(x_ref, xp_ref, xn_ref, gain_ref, win_ref, cw_ref, cb_ref,
                    wa_ref, wx_ref, ba_ref, bx_ref, lam_ref,
                    hf_ref, u_ref, gate_ref, carry_ref):
    i = pl.program_id(1)
    last = pl.num_programs(1) - 1
    d = x_ref.shape[-1]
    gain = gain_ref[...]

    @pl.when(i == 0)
    def _():
        carry_ref[...] = jnp.zeros_like(carry_ref)

    gu = _dot(_rms(x_ref[0], gain), win_ref[...])
    gate_ref[0] = jax.nn.gelu(gu[:, :d])
    u_raw = gu[:, d:]
    before = jnp.where(i > 0, _dot(_rms(xp_ref[0, 0], gain), win_ref[:, d:]), 0.0)
    after = jnp.where(i < last, _dot(_rms(xn_ref[0, 0], gain), win_ref[:, d:]), 0.0)
    prev1, prev2 = _shift_prev(u_raw, [before[SUBLANES - 1:], before[SUBLANES - 2:]])
    nxt = _shift_next(u_raw, after[0:1])
    cw = cw_ref[...]
    u = cw[0:1] * prev2 + cw[1:2] * prev1 + cw[2:3] * u_raw + cw[3:4] * nxt + cb_ref[...]
    u_ref[0] = u
    a, b = _lru_gates(u, wa_ref, wx_ref, ba_ref[...], bx_ref[...], lam_ref[...])
    a_run, h_loc = _tile_scan(a, b, reverse=False)
    hf = h_loc + a_run * carry_ref[...]
    hf_ref[0] = hf
    carry_ref[...] = hf[hf.shape[0] - 1:]


def _lru_bwd_kernel(x_ref, u_ref, gate_ref, hf_ref, wa_ref, wx_ref, ba_ref, bx_ref, lam_ref, wout_ref,
                    o_ref, carry_ref):
    @pl.when(pl.program_id(1) == 0)
    def _():
        carry_ref[...] = jnp.zeros_like(carry_ref)

    a, b = _lru_gates(u_ref[0], wa_ref, wx_ref, ba_ref[...], bx_ref[...], lam_ref[...])
    a_run, h_loc = _tile_scan(a, b, reverse=True)
    hb = h_loc + a_run * carry_ref[...]
    carry_ref[...] = hb[0:1]
    o_ref[0] = x_ref[0] + _dot((hf_ref[0] + hb) * gate_ref[0], wout_ref[...])


def _lru_mixer(x, gain, w_in, conv_w, conv_b, wa, wx, ba, bx, lam, w_out, ts):
    b, s, d = x.shape
    nt = s // ts
    n_groups = s // SUBLANES
    x4 = x.reshape(b, n_groups, SUBLANES, d)
    prev_map, next_map = _halo_specs(ts, d)
    tile = pl.BlockSpec((1, ts, d), lambda bi, i: (bi, i, 0))
    rtile = pl.BlockSpec((1, ts, d), lambda bi, i: (bi, nt - 1 - i, 0))
    halo = (1, 1, SUBLANES, d)

    def whole(a):
        return pl.BlockSpec(a.shape, lambda bi, i, nd=a.ndim: (0,) * nd)

    out = jax.ShapeDtypeStruct((b, s, d), F32)
    fwd_params = (gain, w_in, conv_w, conv_b, wa[0], wx[0], ba[0:1], bx[0:1], lam[0:1])
    hf, u, gate = pl.pallas_call(
        _lru_fwd_kernel,
        grid=(b, nt),
        in_specs=[tile, pl.BlockSpec(halo, prev_map),
                  pl.BlockSpec(halo, functools.partial(next_map, n_groups=n_groups))]
                 + [whole(p) for p in fwd_params],
        out_specs=[tile] * 3,
        out_shape=[out] * 3,
        scratch_shapes=[pltpu.VMEM((1, d), F32)],
        compiler_params=_params("parallel", "arbitrary"),
        name="lru_fwd",
    )(x, x4, x4, *fwd_params)
    bwd_params = (wa[1], wx[1], ba[1:2], bx[1:2], lam[1:2], w_out)
    return pl.pallas_call(
        _lru_bwd_kernel,
        grid=(b, nt),
        in_specs=[rtile] * 4 + [whole(p) for p in bwd_params],
        out_specs=rtile,
        out_shape=out,
        scratch_shapes=[pltpu.VMEM((1, d), F32)],
        compiler_params=_params("parallel", "arbitrary"),
        name="lru_bwd",
    )(x, u, gate, hf, *bwd_params)


def _route_kernel(x_ref, gain_ref, rw_ref, xin_ref, gcol_ref, rank_ref, h_ref, rank_scr, gate_scr, *, cap):
    e = pl.program_id(1)
    n_exp, s = rank_scr.shape

    @pl.when(e == 0)
    def _():
        h = _rms(x_ref[0], gain_ref[...])
        h_ref[...] = h.astype(BF16)
        logits = lax.dot_general(rw_ref[...], h, (((1,), (1,)), ((), ())),
                                 preferred_element_type=F32, precision=HIGHEST)
        ex = jnp.exp(logits - jnp.max(logits, axis=0, keepdims=True))
        aff = ex / jnp.sum(ex, axis=0, keepdims=True)
        def count(pred):
            return jnp.sum(jnp.where(pred, 1.0, 0.0), axis=1, keepdims=True)

        def binade_step(_, jj):
            j_fail, j_ok = jj
            j_mid = jnp.floor(0.5 * (j_fail + j_ok))
            ok = count(aff >= jnp.exp2(1.0 - j_mid)) >= cap
            return jnp.where(ok, j_fail, j_mid), jnp.where(ok, j_mid, j_ok)

        _, j_ok = lax.fori_loop(0, 8, binade_step,
                                (jnp.full((n_exp, 1), -1.0, F32), jnp.full((n_exp, 1), 255.0, F32)))

        def value_step(_, lohi):
            lo, hi = lohi
            mid = 0.5 * (lo + hi)
            ok = count(aff >= mid) >= cap
            return jnp.where(ok, mid, lo), jnp.where(ok, hi, mid)

        lo, hi = lax.fori_loop(0, VALUE_BISECT_STEPS, value_step,
                               (jnp.exp2(1.0 - j_ok), jnp.exp2(2.0 - j_ok)))
        above = aff >= hi
        tied = (aff >= lo) & jnp.logical_not(above)
        need = cap - count(above)
        idx = lax.broadcasted_iota(jnp.int32, (n_exp, s), 1).astype(F32)

        def index_step(_, mm):
            m_fail, m_ok = mm
            m_mid = jnp.floor(0.5 * (m_fail + m_ok))
            ok = count(tied & (idx <= m_mid)) >= need
            return jnp.where(ok, m_fail, m_mid), jnp.where(ok, m_mid, m_ok)

        _, m_ok = lax.fori_loop(0, (s - 1).bit_length(), index_step,
                                (jnp.full((n_exp, 1), -1.0, F32), jnp.full((n_exp, 1), s - 1.0, F32)))
        chosen = above | (tied & (idx <= m_ok))
        gate_scr[...] = jnp.where(chosen, aff, 0.0)
        sel = chosen.astype(F32)
        ri = lax.broadcasted_iota(jnp.int32, (LANES, LANES), 0)
        ci = lax.broadcasted_iota(jnp.int32, (LANES, LANES), 1)
        prefix = (ri <= ci).astype(BF16)
        seen = jnp.zeros((n_exp, 1), F32)
        for j in range(s // LANES):
            sj = sel[:, j * LANES:(j + 1) * LANES]
            pos = jnp.dot(sj.astype(BF16), prefix, preferred_element_type=F32) + seen
            rank_scr[:, j * LANES:(j + 1) * LANES] = jnp.where(sj > 0.0, pos, 0.0)
            seen = seen + jnp.sum(sj, axis=1, keepdims=True)

    rank_row = rank_scr[pl.ds(e, 1), :]
    slot = (lax.broadcasted_iota(jnp.int32, (cap, s), 0) + 1).astype(F32)
    hit = rank_row == slot
    xin_ref[0, 0] = jnp.dot(jnp.where(hit, 1.0, 0.0).astype(BF16), h_ref[...],
                            preferred_element_type=F32).astype(BF16)
    gcol_ref[0, 0] = jnp.sum(jnp.where(hit, gate_scr[pl.ds(e, 1), :], 0.0), axis=1, keepdims=True)
    rank_ref[0, 0] = rank_row


def _route(x, gain, router_wt, cap):
    b, s, d = x.shape
    n_exp = router_wt.shape[0]
    return pl.pallas_call(
        functools.partial(_route_kernel, cap=cap),
        grid=(b, n_exp),
        in_specs=[pl.BlockSpec((1, s, d), lambda bi, e: (bi, 0, 0)),
                  pl.BlockSpec((1, d), lambda bi, e: (0, 0)),
                  pl.BlockSpec((n_exp, d), lambda bi, e: (0, 0))],
        out_specs=[pl.BlockSpec((1, 1, cap, d), lambda bi, e: (bi, e, 0, 0)),
                   pl.BlockSpec((1, 1, cap, 1), lambda bi, e: (bi, e, 0, 0)),
                   pl.BlockSpec((1, 1, 1, s), lambda bi, e: (bi, e, 0, 0))],
        out_shape=[jax.ShapeDtypeStruct((b, n_exp, cap, d), BF16),
                   jax.ShapeDtypeStruct((b, n_exp, cap, 1), F32),
                   jax.ShapeDtypeStruct((b, n_exp, 1, s), F32)],
        scratch_shapes=[pltpu.VMEM((s, d), BF16), pltpu.VMEM((n_exp, s), F32), pltpu.VMEM((n_exp, s), F32)],
        compiler_params=_params("parallel", "arbitrary"),
        name="moe_route",
    )(x, gain, router_wt)


def _ffn_kernel(xin_ref, gcol_ref, wg_ref, wu_ref, wd_ref, o_ref, acc_ref):
    f = pl.program_id(1)
    b, _, cap, d = xin_ref.shape
    xin = xin_ref[...].reshape(b * cap, d)
    hid = jax.nn.silu(_dot(xin, wg_ref[0, 0])) * _dot(xin, wu_ref[0, 0])
    part = _dot(hid, wd_ref[0, 0])

    @pl.when(f == 0)
    def _():
        acc_ref[...] = part

    @pl.when(f > 0)
    def _():
        acc_ref[...] += part

    @pl.when(f == pl.num_programs(1) - 1)
    def _():
        o_ref[...] = (acc_ref[...] * gcol_ref[...].reshape(b * cap, 1)).astype(BF16).reshape(o_ref.shape)


def _ffn(xin, gcol, w_gate, w_up, w_down, layer, tf):
    b, n_exp, cap, d = xin.shape
    ff = w_gate.shape[-1]
    tok = pl.BlockSpec((b, 1, cap, d), lambda e, f: (0, e, 0, 0))
    return pl.pallas_call(
        _ffn_kernel,
        grid=(n_exp, ff // tf),
        in_specs=[tok,
                  pl.BlockSpec((b, 1, cap, 1), lambda e, f: (0, e, 0, 0)),
                  pl.BlockSpec((1, 1, d, tf), lambda e, f: (layer, e, 0, f)),
                  pl.BlockSpec((1, 1, d, tf), lambda e, f: (layer, e, 0, f)),
                  pl.BlockSpec((1, 1, tf, d), lambda e, f: (layer, e, f, 0))],
        out_specs=tok,
        out_shape=jax.ShapeDtypeStruct(xin.shape, BF16),
        scratch_shapes=[pltpu.VMEM((b * cap, d), F32)],
        compiler_params=_params("parallel", "arbitrary"),
        name="moe_ffn",
    )(xin, gcol, w_gate, w_up, w_down)


def _combine_kernel(x_ref, rank_ref, y_ref, gain_ref, o_ref, *, final_norm):
    e = pl.program_id(2)
    cap = y_ref.shape[2]
    ts = o_ref.shape[1]

    @pl.when(e == 0)
    def _():
        o_ref[...] = x_ref[...]

    slot = (lax.broadcasted_iota(jnp.int32, (cap, ts), 0) + 1).astype(F32)
    onehot = jnp.where(rank_ref[0, 0] == slot, 1.0, 0.0)
    o_ref[0] += _dot_tn(onehot, y_ref[0, 0])

    if final_norm:
        @pl.when(e == pl.num_programs(2) - 1)
        def _():
            o_ref[0] = _rms(o_ref[0], gain_ref[...])


def _combine(x, rank, yexp, gain, ts, final_norm):
    b, s, d = x.shape
    n_exp, cap = yexp.shape[1], yexp.shape[2]
    tile = pl.BlockSpec((1, ts, d), lambda bi, i, e: (bi, i, 0))
    return pl.pallas_call(
        functools.partial(_combine_kernel, final_norm=final_norm),
        grid=(b, s // ts, n_exp),
        in_specs=[tile,
                  pl.BlockSpec((1, 1, 1, ts), lambda bi, i, e: (bi, e, 0, i)),
                  pl.BlockSpec((1, 1, cap, d), lambda bi, i, e: (bi, e, 0, 0)),
                  pl.BlockSpec((1, d), lambda bi, i, e: (0, 0))],
        out_specs=tile,
        out_shape=jax.ShapeDtypeStruct((b, s, d), F32),
        compiler_params=_params("parallel", "parallel", "arbitrary"),
        name="moe_combine",
    )(x, rank, yexp, gain)


def _moe(x, gain, router_w, w_gate, w_up, w_down, layer, final_gain):
    b, s, d = x.shape
    cap = CAPACITY_FACTOR * s // N_EXPERTS
    xin, gcol, rank = _route(x, gain, router_w.T, cap)
    yexp = _ffn(xin, gcol, w_gate, w_up, w_down, layer, tf=512)
    final_norm = final_gain is not None
    return _combine(x, rank, yexp, final_gain if final_norm else gain, ts=s // 2, final_norm=final_norm)


def _rwkv_layer(x, gain, mu, w_in, w0, w1, w2, a0, a1, a2, g1, g2, k_k, k_a, r_k, lnx_g, lnx_b, w_out):
    b, s, d = x.shape
    row = lambda t: t.reshape(1, d)
    mu8 = jnp.concatenate([mu, jnp.zeros((SUBLANES - mu.shape[0], d), F32)], axis=0)
    cat_in = lambda t: jnp.concatenate([t[0], t[1]], axis=1).astype(BF16)
    cat_out = lambda t: jnp.concatenate([t[0], t[1]], axis=0).astype(BF16)
    r, k, v, a_f, a_b, lw_f, lw_b, g = _rwkv_proj(
        x, row(gain), mu8, w0, a0, w_in.astype(BF16), cat_in(w1), cat_out(w2), cat_in(a1), cat_out(a2),
        g1.astype(BF16), g2.astype(BF16), ts=256)
    y_f = _rwkv_scan(r, k, v, a_f, lw_f, row(k_k), row(k_a), None, reverse=False, ts=RW_SCAN_TILE)
    y = _rwkv_scan(r, k, v, a_b, lw_b, row(k_k), row(k_a),
                   (y_f, a_f, row(r_k), row(lnx_g), row(lnx_b)), reverse=True, ts=RW_SCAN_TILE)
    t = b * s
    out = _gated_out(x.reshape(t, d), y.reshape(t, d), g.reshape(t, d), w_out.astype(BF16), tm=512)
    return out.reshape(b, s, d)


def kernel(x, norm_mix_g, norm_ffn_g, router_w, exp_w_gate, exp_w_up, exp_w_down, final_norm_g, rw_mu, rw_w_in, rw_w0, rw_w1, rw_w2, rw_a0, rw_a1, rw_a2, rw_g1, rw_g2, rw_k_k, rw_k_a, rw_r_k, rw_lnx_g, rw_lnx_b, rw_w_out, sc_w_in, sc_conv_w, sc_w_out, lru_w_in, lru_conv_w, lru_conv_b, lru_w_a, lru_b_a, lru_w_x, lru_b_x, lru_lambda, lru_w_out):
    depth, d = norm_mix_g.shape
    ia = ib = ic = 0
    for i in range(depth):
        gain = norm_mix_g[i].reshape(1, d)
        m = i % 3
        if m == 0:
            x = _rwkv_layer(x, norm_mix_g[i], rw_mu[ia], rw_w_in[ia], rw_w0[ia], rw_w1[ia], rw_w2[ia],
                            rw_a0[ia], rw_a1[ia], rw_a2[ia], rw_g1[ia], rw_g2[ia], rw_k_k[ia], rw_k_a[ia],
                            rw_r_k[ia], rw_lnx_g[ia], rw_lnx_b[ia], rw_w_out[ia])
            ia += 1
        elif m == 1:
            cw = jnp.concatenate([sc_conv_w[ib], jnp.zeros((SUBLANES - sc_conv_w.shape[1], d), F32)], axis=0)
            x = _sconv_mixer(x, gain, sc_w_in[ib].astype(BF16), cw, sc_w_out[ib].astype(BF16), ts=256)
            ib += 1
        else:
            cw = jnp.concatenate([lru_conv_w[ic], jnp.zeros((SUBLANES - lru_conv_w.shape[1], d), F32)], axis=0)
            x = _lru_mixer(x, gain, lru_w_in[ic].astype(BF16), cw, lru_conv_b[ic].reshape(1, d),
                           lru_w_a[ic].astype(BF16), lru_w_x[ic].astype(BF16), lru_b_a[ic], lru_b_x[ic],
                           lru_lambda[ic], lru_w_out[ic].astype(BF16), ts=256)
            ic += 1
        x = _moe(x, norm_ffn_g[i].reshape(1, d), router_w[i], exp_w_gate, exp_w_up, exp_w_down, i,
                 final_norm_g.reshape(1, d) if i == depth - 1 else None)
    return x
```

```python
import functools

import jax
import jax.numpy as jnp
from jax import lax
from jax.experimental import pallas as pl
from jax.experimental.pallas import tpu as pltpu

F32 = jnp.float32
BF16 = jnp.bfloat16
HIGHEST = lax.Precision.HIGHEST

RMS_EPS = 1e-6
RW_HEAD = 64
RW_LNX_EPS = 64e-5
LRU_HEADS = 4
LRU_C = 8.0
N_EXPERTS = 16
CAPACITY_FACTOR = 2

LANES = 128
SUBLANES = 8
CHUNK = 64
LEAF = 16
RW_SCAN_TILE = 256
RW_SCAN_PAIRS = 2
VALUE_BISECT_STEPS = 48
VMEM_LIMIT_BYTES = 56 * 1024 * 1024


def _params(*semantics):
    return pltpu.CompilerParams(dimension_semantics=semantics, vmem_limit_bytes=VMEM_LIMIT_BYTES)


def _dot(a, b):
    return jnp.dot(a.astype(BF16), b.astype(BF16), preferred_element_type=F32)


def _dot_nt(a, b):
    return lax.dot_general(a.astype(BF16), b.astype(BF16), (((1,), (1,)), ((), ())),
                           preferred_element_type=F32)


def _dot_tn(a, b):
    return lax.dot_general(a.astype(BF16), b.astype(BF16), (((0,), (0,)), ((), ())),
                           preferred_element_type=F32)


def _dot_f32(a, b):
    return jnp.dot(a, b, preferred_element_type=F32, precision=HIGHEST)


def _split3(x):
    hi = x.astype(BF16)
    rest = x - hi.astype(F32)
    mid = rest.astype(BF16)
    return hi, mid, (rest - mid.astype(F32)).astype(BF16)


def _dot_f32_by_01(x, ones01):
    m = ones01.astype(BF16)
    hi, mid, lo = _split3(x)
    return (jnp.dot(hi, m, preferred_element_type=F32)
            + (jnp.dot(mid, m, preferred_element_type=F32) + jnp.dot(lo, m, preferred_element_type=F32)))


def _dot_01_by_f32(ones01, x):
    m = ones01.astype(BF16)
    hi, mid, lo = _split3(x)
    return (jnp.dot(m, hi, preferred_element_type=F32)
            + (jnp.dot(m, mid, preferred_element_type=F32) + jnp.dot(m, lo, preferred_element_type=F32)))


def _rms(x, gain):
    return x * lax.rsqrt(jnp.mean(x * x, axis=-1, keepdims=True) + RMS_EPS) * gain


def _shift_prev(x, first_rows):
    n = x.shape[0]
    row = lax.broadcasted_iota(jnp.int32, x.shape, 0)
    out = []
    for k, edge in enumerate(first_rows, start=1):
        y = pltpu.roll(x, k, 0)
        for j in range(k):
            y = jnp.where(row == j, edge[j:j + 1], y)
        out.append(y)
    return out


def _shift_next(x, after_row):
    n = x.shape[0]
    row = lax.broadcasted_iota(jnp.int32, x.shape, 0)
    return jnp.where(row == n - 1, after_row, pltpu.roll(x, n - 1, 0))


def _halo_specs(ts, d):
    g = ts // SUBLANES

    def prev_map(b, i):
        return (b, jnp.maximum(i * g - 1, 0), 0, 0)

    def next_map(b, i, n_groups):
        return (b, jnp.minimum((i + 1) * g, n_groups - 1), 0, 0)

    return prev_map, next_map


def _rwkv_proj_kernel(x_ref, xp_ref, xn_ref, gain_ref, mu_ref, w0_ref, a0_ref,
                      wrkv_ref, w1_ref, w2_ref, a1_ref, a2_ref, g1_ref, g2_ref,
                      r_ref, k_ref, v_ref, a_fwd_ref, a_bwd_ref, lw_fwd_ref, lw_bwd_ref, g_ref):
    i = pl.program_id(1)
    last = pl.num_programs(1) - 1
    gain = gain_ref[...]
    h = _rms(x_ref[0], gain)
    h_before = jnp.where(i > 0, _rms(xp_ref[0, 0], gain)[SUBLANES - 1:SUBLANES], 0.0)
    h_after = jnp.where(i < last, _rms(xn_ref[0, 0], gain)[0:1], 0.0)
    (prev,) = _shift_prev(h, [h_before])
    nxt = _shift_next(h, h_after)
    xx = 0.5 * (prev + nxt) - h
    mu = mu_ref[...]

    def mix(p):
        return (h + xx * mu[p:p + 1]).astype(BF16)

    r_ref[0] = _dot(mix(0), wrkv_ref[0])
    k_ref[0] = _dot(mix(1), wrkv_ref[1])
    v_ref[0] = _dot(mix(2), wrkv_ref[2])

    tw = jnp.tanh(_dot(mix(3), w1_ref[...]))
    ta = _dot(mix(4), a1_ref[...])
    lane = lax.broadcasted_iota(jnp.int32, tw.shape, 1)
    for z, (lw_ref, a_ref) in enumerate(((lw_fwd_ref, a_fwd_ref), (lw_bwd_ref, a_bwd_ref))):
        mine = (lane // RW_HEAD) == z
        lora_w = _dot(jnp.where(mine, tw, 0.0), w2_ref[...])
        w_log = -jax.nn.softplus(-(w0_ref[z:z + 1] + lora_w)) - 0.5
        lw_ref[0] = -jnp.exp(w_log)
        lora_a = _dot(jnp.where(mine, ta, 0.0), a2_ref[...])
        a_ref[0] = jax.nn.sigmoid(a0_ref[z:z + 1] + lora_a)
    g_ref[0] = _dot(jax.nn.sigmoid(_dot(mix(5), g1_ref[...])), g2_ref[...])


def _rwkv_proj(x, gain, mu, w0, a0, wrkv, w1, w2, a1, a2, g1, g2, ts):
    b, s, d = x.shape
    nt = s // ts
    n_groups = s // SUBLANES
    x4 = x.reshape(b, n_groups, SUBLANES, d)
    prev_map, next_map = _halo_specs(ts, d)
    tile = pl.BlockSpec((1, ts, d), lambda bi, i: (bi, i, 0))
    halo = (1, 1, SUBLANES, d)

    def whole(a):
        return pl.BlockSpec(a.shape, lambda bi, i, nd=a.ndim: (0,) * nd)

    params = (gain, mu, w0, a0, wrkv, w1, w2, a1, a2, g1, g2)
    out = jax.ShapeDtypeStruct((b, s, d), F32)
    return pl.pallas_call(
        _rwkv_proj_kernel,
        grid=(b, nt),
        in_specs=[tile,
                  pl.BlockSpec(halo, prev_map),
                  pl.BlockSpec(halo, functools.partial(next_map, n_groups=n_groups))]
                 + [whole(p) for p in params],
        out_specs=[tile] * 8,
        out_shape=[out] * 8,
        compiler_params=_params("parallel", "parallel"),
        name="rwkv_proj",
    )(x, x4, x4, *params)


def _rwkv_scan_kernel(*refs, reverse, final, n_chunks):
    if final:
        (r_ref, k_ref, v_ref, a_ref, lw_ref, kk_scale_ref, ka_ref,
         yf_ref, a_other_ref, rk_ref, lng_ref, lnb_ref, y_ref, state_ref) = refs
    else:
        (r_ref, k_ref, v_ref, a_ref, lw_ref, kk_scale_ref, ka_ref, y_ref, state_ref) = refs

    @pl.when(pl.program_id(2) == 0)
    def _():
        state_ref[...] = jnp.zeros_like(state_ref)

    c, w2 = CHUNK, 2 * RW_HEAD
    row = lax.broadcasted_iota(jnp.int32, (c, w2), 0)
    lane = lax.broadcasted_iota(jnp.int32, (c, w2), 1)
    col = lane % RW_HEAD
    head0 = lane < RW_HEAD
    if reverse:
        strict, incl = col > row, col >= row
    else:
        strict, incl = col < row, col <= row
    leaf = (col // LEAF) == (row // LEAF)
    eye_pair = (col == row).astype(F32)
    rr = lax.broadcasted_iota(jnp.int32, (c, c), 0)
    cc = lax.broadcasted_iota(jnp.int32, (c, c), 1)
    cum_mat = ((cc >= rr) if reverse else (cc <= rr)).astype(F32)
    r2 = lax.broadcasted_iota(jnp.int32, (w2, w2), 0)
    c2 = lax.broadcasted_iota(jnp.int32, (w2, w2), 1)
    same_head = (r2 // RW_HEAD) == (c2 // RW_HEAD)
    head_ones = same_head.astype(F32)
    eye_w2 = r2 == c2

    def bd(m):
        return jnp.concatenate([jnp.where(head0, m, 0.0), jnp.where(head0, 0.0, m)], axis=0)

    def pm(a, m):
        return _dot(a, bd(m))

    n_pairs = r_ref.shape[-1] // w2
    order = list(range(n_chunks - 1, -1, -1) if reverse else range(n_chunks))
    items = [(pl.ds(j * c, c), slice(p * w2, (p + 1) * w2)) for p in range(n_pairs) for j in order]

    def each(fn, *lists):
        return [fn(*args) for args in zip(*lists)]

    def tile(ref):
        return [ref[0, rw, ls] for rw, ls in items]

    def vec(ref):
        return [ref[:, ls] for _, ls in items]

    r, k, v, a, lw = tile(r_ref), tile(k_ref), tile(v_ref), tile(a_ref), tile(lw_ref)
    ka = vec(ka_ref)
    kk = each(lambda x, sc: x * sc, k, vec(kk_scale_ref))
    sumsq = each(lambda x: _dot_f32_by_01(x * x, head_ones), kk)
    cum = each(lambda x: _dot_01_by_f32(cum_mat, x), lw)
    kap = each(lambda x, ss: x / jnp.maximum(jnp.sqrt(ss), 1e-12), kk, sumsq)
    kd = each(lambda x, aa, kav: x * (1.0 + (aa - 1.0) * kav), k, a, ka)
    beta = each(lambda aa, kp: aa * kp, a, kap)
    total = each(lambda x: x[0:1] if reverse else x[c - 1:c], cum)
    g_out = each(lambda x: jnp.exp(-x), cum)
    g_tail = each(lambda t, x: jnp.exp(t - x), total, cum)
    r_s = each(lambda x, cm: x * jnp.exp(cm), r, cum)
    kap_s = each(lambda x, cm, l: x * jnp.exp(cm - l), kap, cum, lw)
    k_s = each(lambda x, g: x * g, kd, g_out)
    b_s = each(lambda x, g: x * g, beta, g_out)
    k_hat = each(lambda x, g: x * g, kd, g_tail)
    b_hat = each(lambda x, g: x * g, beta, g_tail)

    def gram_fn(kp, rs, ks, bs):
        stacked = jnp.concatenate([jnp.where(head0, ks, 0.0), jnp.where(head0, 0.0, ks),
                                   jnp.where(head0, bs, 0.0), jnp.where(head0, 0.0, bs)], axis=0)
        return _dot_nt(jnp.concatenate([kp, rs], axis=0), stacked)

    gram = each(gram_fn, kap_s, r_s, k_s, b_s)
    a_kk = each(lambda g: jnp.where(strict, g[:c, :w2], 0.0), gram)
    n_mat = each(lambda g: jnp.where(strict, g[:c, w2:], 0.0), gram)
    a_rk = each(lambda g: jnp.where(incl, g[c:, :w2], 0.0), gram)
    a_rb = each(lambda g: jnp.where(incl, g[c:, w2:], 0.0), gram)

    n_d = each(lambda n: jnp.where(leaf, n, 0.0), n_mat)
    n_off = each(lambda n, nd: n - nd, n_mat, n_d)
    n2 = each(pm, n_d, n_d)
    n4 = each(pm, n2, n2)
    n8 = each(pm, n4, n4)
    t_d = each(lambda x, y: pm(eye_pair - x, eye_pair + y), n_d, n2)
    t_d = each(lambda x, y: pm(x, eye_pair + y), t_d, n4)
    t_d = each(lambda x, y: pm(x, eye_pair + y), t_d, n8)
    e1 = each(pm, t_d, n_off)
    e2 = each(pm, e1, e1)
    t_mat = each(lambda x, y: pm(eye_pair - x, eye_pair + y), e1, e2)
    t_mat = each(pm, t_mat, t_d)

    v_bd = each(bd, v)
    av = each(_dot, a_kk, v_bd)
    mw = each(lambda t, kp, x: _dot(t, jnp.concatenate([bd(kp), bd(x)], axis=1)), t_mat, kap_s, av)
    m_mat = each(lambda x: x[:, :w2], mw)
    w_mat = each(lambda x: x[:, w2:], mw)
    zeros_bd = jnp.zeros((w2, w2), F32)

    def qy_fn(ark, arb, vb, m, w):
        return _dot(jnp.concatenate([ark, arb], axis=1),
                    jnp.concatenate([jnp.concatenate([zeros_bd, vb], axis=1),
                                     jnp.concatenate([-bd(m), -bd(w)], axis=1)], axis=0))

    def ph_fn(kh, bh, vv, m, w):
        return _dot_tn(jnp.concatenate([kh, bh], axis=0),
                       jnp.concatenate([jnp.concatenate([jnp.zeros_like(vv), vv], axis=1),
                                        jnp.concatenate([-m, -w], axis=1)], axis=0))

    qy = each(qy_fn, a_rk, a_rb, v_bd, m_mat, w_mat)
    ph = each(ph_fn, k_hat, b_hat, v, m_mat, w_mat)
    q_mat = each(lambda rs, x: rs + x[:, :w2], r_s, qy)
    y_intra = each(lambda x: x[:, w2:], qy)
    p_mat = each(lambda x, t: jnp.where(same_head, x[:, :w2], 0.0) + jnp.where(eye_w2, jnp.exp(t), 0.0),
                 ph, total)
    h_intra = each(lambda x: jnp.where(same_head, x[:, w2:], 0.0), ph)

    y = []
    for p in range(n_pairs):
        state = state_ref[p]
        for i in range(p * n_chunks, (p + 1) * n_chunks):
            step = _dot(jnp.concatenate([q_mat[i], p_mat[i]], axis=0), state)
            y.append(step[:c] + y_intra[i])
            state = step[c:] + h_intra[i]
        state_ref[p] = state

    if final:
        y = each(lambda x, yf: x + yf, y, tile(yf_ref))
        mean = each(lambda x: _dot_f32_by_01(x, head_ones) * (1.0 / RW_HEAD), y)
        yc = each(lambda x, m: x - m, y, mean)
        var = each(lambda x: _dot_f32_by_01(x * x, head_ones) * (1.0 / RW_HEAD), yc)
        kd_other = each(lambda x, ao, kav: x * (1.0 + (ao - 1.0) * kav), k, tile(a_other_ref), ka)
        bonus = each(lambda rr_, x, xo, rk: _dot_f32_by_01(rr_ * (x + xo) * rk, head_ones),
                     r, kd, kd_other, vec(rk_ref))
        y = each(lambda x, vr, bo, vv, lg, lb: x * lax.rsqrt(vr + RW_LNX_EPS) * lg + lb + bo * vv,
                 yc, var, bonus, v, vec(lng_ref), vec(lnb_ref))
    for (rw, ls), x in zip(items, y):
        y_ref[0, rw, ls] = x


def _rwkv_scan(r, k, v, a, lw, kk_scale, ka, extra, *, reverse, ts):
    b, s, d = r.shape
    nt = s // ts
    w2 = 2 * RW_HEAD
    wb = RW_SCAN_PAIRS * w2
    if reverse:
        tile = pl.BlockSpec((1, ts, wb), lambda bi, hp, i: (bi, nt - 1 - i, hp))
    else:
        tile = pl.BlockSpec((1, ts, wb), lambda bi, hp, i: (bi, i, hp))
    vec = pl.BlockSpec((1, wb), lambda bi, hp, i: (0, hp))
    final = extra is not None
    args = [r, k, v, a, lw, kk_scale, ka]
    specs = [tile] * 5 + [vec] * 2
    if final:
        yf, a_other, rk, lng, lnb = extra
        args += [yf, a_other, rk, lng, lnb]
        specs += [tile] * 2 + [vec] * 3
    return pl.pallas_call(
        functools.partial(_rwkv_scan_kernel, reverse=reverse, final=final, n_chunks=ts // CHUNK),
        grid=(b, d // wb, nt),
        in_specs=specs,
        out_specs=tile,
        out_shape=jax.ShapeDtypeStruct((b, s, d), F32),
        scratch_shapes=[pltpu.VMEM((RW_SCAN_PAIRS, w2, w2), F32)],
        compiler_params=_params("parallel", "parallel", "arbitrary"),
        name="rwkv_scan_bwd" if reverse else "rwkv_scan_fwd",
    )(*args)


def _gated_out_kernel(x_ref, y_ref, g_ref, w_ref, o_ref):
    o_ref[...] = x_ref[...] + _dot(y_ref[...] * g_ref[...], w_ref[...])


def _gated_out(x, y, g, w, tm):
    t, d = x.shape
    tile = pl.BlockSpec((tm, d), lambda i: (i, 0))
    return pl.pallas_call(
        _gated_out_kernel,
        grid=(t // tm,),
        in_specs=[tile, tile, tile, pl.BlockSpec(w.shape, lambda i: (0, 0))],
        out_specs=tile,
        out_shape=jax.ShapeDtypeStruct((t, d), F32),
        compiler_params=_params("parallel"),
        name="gated_out",
    )(x, y, g, w)


def _sconv_kernel(x_ref, xp_ref, xn_ref, gain_ref, win_ref, cw_ref, wout_ref, o_ref):
    i = pl.program_id(1)
    last = pl.num_programs(1) - 1
    d = x_ref.shape[-1]
    gain = gain_ref[...]
    x = x_ref[0]
    bcu = _dot(_rms(x, gain), win_ref[...])
    cu = bcu[:, d:2 * d] * bcu[:, 2 * d:]

    def edge_cu(ref):
        e = _dot(_rms(ref[0, 0], gain), win_ref[:, d:])
        return e[:, :d] * e[:, d:]

    cu_before = jnp.where(i > 0, edge_cu(xp_ref)[SUBLANES - 1:SUBLANES], 0.0)
    cu_after = jnp.where(i < last, edge_cu(xn_ref)[0:1], 0.0)
    (prev,) = _shift_prev(cu, [cu_before])
    nxt = _shift_next(cu, cu_after)
    cw = cw_ref[...]
    z = cw[0:1] * prev + cw[1:2] * cu + cw[2:3] * nxt
    o_ref[0] = x + _dot(bcu[:, :d] * z, wout_ref[...])


def _sconv_mixer(x, gain, w_in, conv_w, w_out, ts):
    b, s, d = x.shape
    n_groups = s // SUBLANES
    x4 = x.reshape(b, n_groups, SUBLANES, d)
    prev_map, next_map = _halo_specs(ts, d)
    tile = pl.BlockSpec((1, ts, d), lambda bi, i: (bi, i, 0))
    halo = (1, 1, SUBLANES, d)

    def whole(a):
        return pl.BlockSpec(a.shape, lambda bi, i, nd=a.ndim: (0,) * nd)

    params = (gain, w_in, conv_w, w_out)
    return pl.pallas_call(
        _sconv_kernel,
        grid=(b, s // ts),
        in_specs=[tile, pl.BlockSpec(halo, prev_map),
                  pl.BlockSpec(halo, functools.partial(next_map, n_groups=n_groups))]
                 + [whole(p) for p in params],
        out_specs=tile,
        out_shape=jax.ShapeDtypeStruct((b, s, d), F32),
        compiler_params=_params("parallel", "parallel"),
        name="sconv_mixer",
    )(x, x4, x4, *params)


def _tile_scan(a, b, reverse):
    n = a.shape[0]
    row = lax.broadcasted_iota(jnp.int32, a.shape, 0)
    s = 1
    while s < n:
        if reverse:
            a_far, b_far, valid = pltpu.roll(a, n - s, 0), pltpu.roll(b, n - s, 0), row < n - s
        else:
            a_far, b_far, valid = pltpu.roll(a, s, 0), pltpu.roll(b, s, 0), row >= s
        b = jnp.where(valid, a * b_far + b, b)
        a = jnp.where(valid, a * a_far, a)
        s *= 2
    return a, b


def _lru_gates(u, wa_ref, wx_ref, ba, bx, lam):
    d = u.shape[-1]
    blk = d // LRU_HEADS
    ub = u.astype(BF16)
    r_parts, i_parts = [], []
    for hh in range(LRU_HEADS):
        uh = ub[:, hh * blk:(hh + 1) * blk]
        r_parts.append(jnp.dot(uh, wa_ref[hh], preferred_element_type=F32))
        i_parts.append(jnp.dot(uh, wx_ref[hh], preferred_element_type=F32))
    r_gate = jax.nn.sigmoid(jnp.concatenate(r_parts, axis=1) + ba)
    i_gate = jax.nn.sigmoid(jnp.concatenate(i_parts, axis=1) + bx)
    log_a = -LRU_C * r_gate * jax.nn.softplus(-lam)
    th = jnp.tanh(log_a)
    return jnp.exp(log_a), jnp.sqrt(-2.0 * th / (1.0 - th)) * (i_gate * u)


def _lru_fwd_kernel(x_ref, xp_ref, xn_ref, gain_ref, win_ref, cw_ref, cb_ref,
                    wa_ref, wx_ref, ba_ref, bx_ref, lam_ref,
                    hf_ref, u_ref, gate_ref, carry_ref):
    i = pl.program_id(1)
    last = pl.num_programs(1) - 1
    d = x_ref.shape[-1]
    gain = gain_ref[...]

    @pl.when(i == 0)
    def _():
        carry_ref[...] = jnp.zeros_like(carry_ref)

    gu = _dot(_rms(x_ref[0], gain), win_ref[...])
    gate_ref[0] = jax.nn.gelu(gu[:, :d])
    u_raw = gu[:, d:]
    before = jnp.where(i > 0, _dot(_rms(xp_ref[0, 0], gain), win_ref[:, d:]), 0.0)
    after = jnp.where(i < last, _dot(_rms(xn_ref[0, 0], gain), win_ref[:, d:]), 0.0)
    prev1, prev2 = _shift_prev(u_raw, [before[SUBLANES - 1:], before[SUBLANES - 2:]])
    nxt = _shift_next(u_raw, after[0:1])
    cw = cw_ref[...]
    u = cw[0:1] * prev2 + cw[1:2] * prev1 + cw[2:3] * u_raw + cw[3:4] * nxt + cb_ref[...]
    u_ref[0] = u
    a, b = _lru_gates(u, wa_ref, wx_ref, ba_ref[...], bx_ref[...], lam_ref[...])
    a_run, h_loc = _tile_scan(a, b, reverse=False)
    hf = h_loc + a_run * carry_ref[...]
    hf_ref[0] = hf
    carry_ref[...] = hf[hf.shape[0] - 1:]


def _lru_bwd_kernel(x_ref, u_ref, gate_ref, hf_ref, wa_ref, wx_ref, ba_ref, bx_ref, lam_ref, wout_ref,
                    o_ref, carry_ref):
    @pl.when(pl.program_id(1) == 0)
    def _():
        carry_ref[...] = jnp.zeros_like(carry_ref)

    a, b = _lru_gates(u_ref[0], wa_ref, wx_ref, ba_ref[...], bx_ref[...], lam_ref[...])
    a_run, h_loc = _tile_scan(a, b, reverse=True)
    hb = h_loc + a_run * carry_ref[...]
    carry_ref[...] = hb[0:1]
    o_ref[0] = x_ref[0] + _dot((hf_ref[0] + hb) * gate_ref[0], wout_ref[...])


def _lru_mixer(x, gain, w_in, conv_w, conv_b, wa, wx, ba, bx, lam, w_out, ts):
    b, s, d = x.shape
    nt = s // ts
    n_groups = s // SUBLANES
    x4 = x.reshape(b, n_groups, SUBLANES, d)
    prev_map, next_map = _halo_specs(ts, d)
    tile = pl.BlockSpec((1, ts, d), lambda bi, i: (bi, i, 0))
    rtile = pl.BlockSpec((1, ts, d), lambda bi, i: (bi, nt - 1 - i, 0))
    halo = (1, 1, SUBLANES, d)

    def whole(a):
        return pl.BlockSpec(a.shape, lambda bi, i, nd=a.ndim: (0,) * nd)

    out = jax.ShapeDtypeStruct((b, s, d), F32)
    fwd_params = (gain, w_in, conv_w, conv_b, wa[0], wx[0], ba[0:1], bx[0:1], lam[0:1])
    hf, u, gate = pl.pallas_call(
        _lru_fwd_kernel,
        grid=(b, nt),
        in_specs=[tile, pl.BlockSpec(halo, prev_map),
                  pl.BlockSpec(halo, functools.partial(next_map, n_groups=n_groups))]
                 + [whole(p) for p in fwd_params],
        out_specs=[tile] * 3,
        out_shape=[out] * 3,
        scratch_shapes=[pltpu.VMEM((1, d), F32)],
        compiler_params=_params("parallel", "arbitrary"),
        name="lru_fwd",
    )(x, x4, x4, *fwd_params)
    bwd_params = (wa[1], wx[1], ba[1:2], bx[1:2], lam[1:2], w_out)
    return pl.pallas_call(
        _lru_bwd_kernel,
        grid=(b, nt),
        in_specs=[rtile] * 4 + [whole(p) for p in bwd_params],
        out_specs=rtile,
        out_shape=out,
        scratch_shapes=[pltpu.VMEM((1, d), F32)],
        compiler_params=_params("parallel", "arbitrary"),
        name="lru_bwd",
    )(x, u, gate, hf, *bwd_params)


def _route_kernel(x_ref, gain_ref, rw_ref, xin_ref, gcol_ref, rank_ref, h_ref, rank_scr, gate_scr, *, cap):
    e = pl.program_id(1)
    n_exp, s = rank_scr.shape

    @pl.when(e == 0)
    def _():
        h = _rms(x_ref[0], gain_ref[...])
        h_ref[...] = h.astype(BF16)
        logits = lax.dot_general(rw_ref[...], h, (((1,), (1,)), ((), ())),
                                 preferred_element_type=F32, precision=HIGHEST)
        ex = jnp.exp(logits - jnp.max(logits, axis=0, keepdims=True))
        aff = ex / jnp.sum(ex, axis=0, keepdims=True)
        def count(pred):
            return jnp.sum(jnp.where(pred, 1.0, 0.0), axis=1, keepdims=True)

        def binade_step(_, jj):
            j_fail, j_ok = jj
            j_mid = jnp.floor(0.5 * (j_fail + j_ok))
            ok = count(aff >= jnp.exp2(1.0 - j_mid)) >= cap
            return jnp.where(ok, j_fail, j_mid), jnp.where(ok, j_mid, j_ok)

        _, j_ok = lax.fori_loop(0, 8, binade_step,
                                (jnp.full((n_exp, 1), -1.0, F32), jnp.full((n_exp, 1), 255.0, F32)))

        def value_step(_, lohi):
            lo, hi = lohi
            mid = 0.5 * (lo + hi)
            ok = count(aff >= mid) >= cap
            return jnp.where(ok, mid, lo), jnp.where(ok, hi, mid)

        lo, hi = lax.fori_loop(0, VALUE_BISECT_STEPS, value_step,
                               (jnp.exp2(1.0 - j_ok), jnp.exp2(2.0 - j_ok)))
        above = aff >= hi
        tied = (aff >= lo) & jnp.logical_not(above)
        need = cap - count(above)
        idx = lax.broadcasted_iota(jnp.int32, (n_exp, s), 1).astype(F32)

        def index_step(_, mm):
            m_fail, m_ok = mm
            m_mid = jnp.floor(0.5 * (m_fail + m_ok))
            ok = count(tied & (idx <= m_mid)) >= need
            return jnp.where(ok, m_fail, m_mid), jnp.where(ok, m_mid, m_ok)

        _, m_ok = lax.fori_loop(0, (s - 1).bit_length(), index_step,
                                (jnp.full((n_exp, 1), -1.0, F32), jnp.full((n_exp, 1), s - 1.0, F32)))
        chosen = above | (tied & (idx <= m_ok))
        gate_scr[...] = jnp.where(chosen, aff, 0.0)
        sel = chosen.astype(F32)
        ri = lax.broadcasted_iota(jnp.int32, (LANES, LANES), 0)
        ci = lax.broadcasted_iota(jnp.int32, (LANES, LANES), 1)
        prefix = (ri <= ci).astype(BF16)
        seen = jnp.zeros((n_exp, 1), F32)
        for j in range(s // LANES):
            sj = sel[:, j * LANES:(j + 1) * LANES]
            pos = jnp.dot(sj.astype(BF16), prefix, preferred_element_type=F32) + seen
            rank_scr[:, j * LANES:(j + 1) * LANES] = jnp.where(sj > 0.0, pos, 0.0)
            seen = seen + jnp.sum(sj, axis=1, keepdims=True)

    rank_row = rank_scr[pl.ds(e, 1), :]
    slot = (lax.broadcasted_iota(jnp.int32, (cap, s), 0) + 1).astype(F32)
    hit = rank_row == slot
    xin_ref[0, 0] = jnp.dot(jnp.where(hit, 1.0, 0.0).astype(BF16), h_ref[...],
                            preferred_element_type=F32).astype(BF16)
    gcol_ref[0, 0] = jnp.sum(jnp.where(hit, gate_scr[pl.ds(e, 1), :], 0.0), axis=1, keepdims=True)
    rank_ref[0, 0] = rank_row


def _route(x, gain, router_wt, cap):
    b, s, d = x.shape
    n_exp = router_wt.shape[0]
    return pl.pallas_call(
        functools.partial(_route_kernel, cap=cap),
        grid=(b, n_exp),
        in_specs=[pl.BlockSpec((1, s, d), lambda bi, e: (bi, 0, 0)),
                  pl.BlockSpec((1, d), lambda bi, e: (0, 0)),
                  pl.BlockSpec((n_exp, d), lambda bi, e: (0, 0))],
        out_specs=[pl.BlockSpec((1, 1, cap, d), lambda bi, e: (bi, e, 0, 0)),
                   pl.BlockSpec((1, 1, cap, 1), lambda bi, e: (bi, e, 0, 0)),
                   pl.BlockSpec((1, 1, 1, s), lambda bi, e: (bi, e, 0, 0))],
        out_shape=[jax.ShapeDtypeStruct((b, n_exp, cap, d), BF16),
                   jax.ShapeDtypeStruct((b, n_exp, cap, 1), F32),
                   jax.ShapeDtypeStruct((b, n_exp, 1, s), F32)],
        scratch_shapes=[pltpu.VMEM((s, d), BF16), pltpu.VMEM((n_exp, s), F32), pltpu.VMEM((n_exp, s), F32)],
        compiler_params=_params("parallel", "arbitrary"),
        name="moe_route",
    )(x, gain, router_wt)


def _ffn_kernel(xin_ref, gcol_ref, wg_ref, wu_ref, wd_ref, o_ref, acc_ref):
    f = pl.program_id(1)
    b, _, cap, d = xin_ref.shape
    xin = xin_ref[...].reshape(b * cap, d)
    hid = jax.nn.silu(_dot(xin, wg_ref[0, 0])) * _dot(xin, wu_ref[0, 0])
    part = _dot(hid, wd_ref[0, 0])

    @pl.when(f == 0)
    def _():
        acc_ref[...] = part

    @pl.when(f > 0)
    def _():
        acc_ref[...] += part

    @pl.when(f == pl.num_programs(1) - 1)
    def _():
        o_ref[...] = (acc_ref[...] * gcol_ref[...].reshape(b * cap, 1)).astype(BF16).reshape(o_ref.shape)


def _ffn(xin, gcol, w_gate, w_up, w_down, layer, tf):
    b, n_exp, cap, d = xin.shape
    ff = w_gate.shape[-1]
    tok = pl.BlockSpec((b, 1, cap, d), lambda e, f: (0, e, 0, 0))
    return pl.pallas_call(
        _ffn_kernel,
        grid=(n_exp, ff // tf),
        in_specs=[tok,
                  pl.BlockSpec((b, 1, cap, 1), lambda e, f: (0, e, 0, 0)),
                  pl.BlockSpec((1, 1, d, tf), lambda e, f: (layer, e, 0, f)),
                  pl.BlockSpec((1, 1, d, tf), lambda e, f: (layer, e, 0, f)),
                  pl.BlockSpec((1, 1, tf, d), lambda e, f: (layer, e, f, 0))],
        out_specs=tok,
        out_shape=jax.ShapeDtypeStruct(xin.shape, BF16),
        scratch_shapes=[pltpu.VMEM((b * cap, d), F32)],
        compiler_params=_params("parallel", "arbitrary"),
        name="moe_ffn",
    )(xin, gcol, w_gate, w_up, w_down)


def _combine_kernel(x_ref, rank_ref, y_ref, gain_ref, o_ref, *, final_norm):
    e = pl.program_id(2)
    cap = y_ref.shape[2]
    ts = o_ref.shape[1]

    @pl.when(e == 0)
    def _():
        o_ref[...] = x_ref[...]

    slot = (lax.broadcasted_iota(jnp.int32, (cap, ts), 0) + 1).astype(F32)
    onehot = jnp.where(rank_ref[0, 0] == slot, 1.0, 0.0)
    o_ref[0] += _dot_tn(onehot, y_ref[0, 0])

    if final_norm:
        @pl.when(e == pl.num_programs(2) - 1)
        def _():
            o_ref[0] = _rms(o_ref[0], gain_ref[...])


def _combine(x, rank, yexp, gain, ts, final_norm):
    b, s, d = x.shape
    n_exp, cap = yexp.shape[1], yexp.shape[2]
    tile = pl.BlockSpec((1, ts, d), lambda bi, i, e: (bi, i, 0))
    return pl.pallas_call(
        functools.partial(_combine_kernel, final_norm=final_norm),
        grid=(b, s // ts, n_exp),
        in_specs=[tile,
                  pl.BlockSpec((1, 1, 1, ts), lambda bi, i, e: (bi, e, 0, i)),
                  pl.BlockSpec((1, 1, cap, d), lambda bi, i, e: (bi, e, 0, 0)),
                  pl.BlockSpec((1, d), lambda bi, i, e: (0, 0))],
        out_specs=tile,
        out_shape=jax.ShapeDtypeStruct((b, s, d), F32),
        compiler_params=_params("parallel", "parallel", "arbitrary"),
        name="moe_combine",
    )(x, rank, yexp, gain)


def _moe(x, gain, router_w, w_gate, w_up, w_down, layer, final_gain):
    b, s, d = x.shape
    cap = CAPACITY_FACTOR * s // N_EXPERTS
    xin, gcol, rank = _route(x, gain, router_w.T, cap)
    yexp = _ffn(xin, gcol, w_gate, w_up, w_down, layer, tf=512)
    final_norm = final_gain is not None
    return _combine(x, rank, yexp, final_gain if final_norm else gain, ts=s // 2, final_norm=final_norm)


def _rwkv_layer(x, gain, mu, w_in, w0, w1, w2, a0, a1, a2, g1, g2, k_k, k_a, r_k, lnx_g, lnx_b, w_out):
    b, s, d = x.shape
    row = lambda t: t.reshape(1, d)
    mu8 = jnp.concatenate([mu, jnp.zeros((SUBLANES - mu.shape[0], d), F32)], axis=0)
    cat_in = lambda t: jnp.concatenate([t[0], t[1]], axis=1).astype(BF16)
    cat_out = lambda t: jnp.concatenate([t[0], t[1]], axis=0).astype(BF16)
    r, k, v, a_f, a_b, lw_f, lw_b, g = _rwkv_proj(
        x, row(gain), mu8, w0, a0, w_in.astype(BF16), cat_in(w1), cat_out(w2), cat_in(a1), cat_out(a2),
        g1.astype(BF16), g2.astype(BF16), ts=256)
    y_f = _rwkv_scan(r, k, v, a_f, lw_f, row(k_k), row(k_a), None, reverse=False, ts=RW_SCAN_TILE)
    y = _rwkv_scan(r, k, v, a_b, lw_b, row(k_k), row(k_a),
                   (y_f, a_f, row(r_k), row(lnx_g), row(lnx_b)), reverse=True, ts=RW_SCAN_TILE)
    t = b * s
    out = _gated_out(x.reshape(t, d), y.reshape(t, d), g.reshape(t, d), w_out.astype(BF16), tm=512)
    return out.reshape(b, s, d)


def kernel(x, norm_mix_g, norm_ffn_g, router_w, exp_w_gate, exp_w_up, exp_w_down, final_norm_g, rw_mu, rw_w_in, rw_w0, rw_w1, rw_w2, rw_a0, rw_a1, rw_a2, rw_g1, rw_g2, rw_k_k, rw_k_a, rw_r_k, rw_lnx_g, rw_lnx_b, rw_w_out, sc_w_in, sc_conv_w, sc_w_out, lru_w_in, lru_conv_w, lru_conv_b, lru_w_a, lru_b_a, lru_w_x, lru_b_x, lru_lambda, lru_w_out):
    depth, d = norm_mix_g.shape
    ia = ib = ic = 0
    for i in range(depth):
        gain = norm_mix_g[i].reshape(1, d)
        m = i % 3
        if m == 0:
            x = _rwkv_layer(x, norm_mix_g[i], rw_mu[ia], rw_w_in[ia], rw_w0[ia], rw_w1[ia], rw_w2[ia],
                            rw_a0[ia], rw_a1[ia], rw_a2[ia], rw_g1[ia], rw_g2[ia], rw_k_k[ia], rw_k_a[ia],
                            rw_r_k[ia], rw_lnx_g[ia], rw_lnx_b[ia], rw_w_out[ia])
            ia += 1
        elif m == 1:
            cw = jnp.concatenate([sc_conv_w[ib], jnp.zeros((SUBLANES - sc_conv_w.shape[1], d), F32)], axis=0)
            x = _sconv_mixer(x, gain, sc_w_in[ib].astype(BF16), cw, sc_w_out[ib].astype(BF16), ts=256)
            ib += 1
        else:
            cw = jnp.concatenate([lru_conv_w[ic], jnp.zeros((SUBLANES - lru_conv_w.shape[1], d), F32)], axis=0)
            x = _lru_mixer(x, gain, lru_w_in[ic].astype(BF16), cw, lru_conv_b[ic].reshape(1, d),
                           lru_w_a[ic].astype(BF16), lru_w_x[ic].astype(BF16), lru_b_a[ic], lru_b_x[ic],
                           lru_lambda[ic], lru_w_out[ic].astype(BF16), ts=256)
            ic += 1
        x = _moe(x, norm_ffn_g[i].reshape(1, d), router_w[i], exp_w_gate, exp_w_up, exp_w_down, i,
                 final_norm_g.reshape(1, d) if i == depth - 1 else None)
    return x
```

```python
import functools

import jax
import jax.numpy as jnp
from jax import lax
from jax.experimental import pallas as pl
from jax.experimental.pallas import tpu as pltpu

F32 = jnp.float32
BF16 = jnp.bfloat16
HIGHEST = lax.Precision.HIGHEST

RMS_EPS = 1e-6
RW_HEAD = 64
RW_LNX_EPS = 64e-5
LRU_HEADS = 4
LRU_C = 8.0
N_EXPERTS = 16
CAPACITY_FACTOR = 2

LANES = 128
SUBLANES = 8
CHUNK = 64
LEAF = 16
RW_SCAN_TILE = 512
RW_SCAN_PAIRS = 2
VALUE_BISECT_STEPS = 48
VMEM_LIMIT_BYTES = 56 * 1024 * 1024


def _params(*semantics):
    return pltpu.CompilerParams(dimension_semantics=semantics, vmem_limit_bytes=VMEM_LIMIT_BYTES)


def _dot(a, b):
    return jnp.dot(a.astype(BF16), b.astype(BF16), preferred_element_type=F32)


def _dot_nt(a, b):
    return lax.dot_general(a.astype(BF16), b.astype(BF16), (((1,), (1,)), ((), ())),
                           preferred_element_type=F32)


def _dot_tn(a, b):
    return lax.dot_general(a.astype(BF16), b.astype(BF16), (((0,), (0,)), ((), ())),
                           preferred_element_type=F32)


def _dot_f32(a, b):
    return jnp.dot(a, b, preferred_element_type=F32, precision=HIGHEST)


def _split3(x):
    hi = x.astype(BF16)
    rest = x - hi.astype(F32)
    mid = rest.astype(BF16)
    return hi, mid, (rest - mid.astype(F32)).astype(BF16)


def _dot_f32_by_01(x, ones01):
    m = ones01.astype(BF16)
    hi, mid, lo = _split3(x)
    return (jnp.dot(hi, m, preferred_element_type=F32)
            + (jnp.dot(mid, m, preferred_element_type=F32) + jnp.dot(lo, m, preferred_element_type=F32)))


def _dot_01_by_f32(ones01, x):
    m = ones01.astype(BF16)
    hi, mid, lo = _split3(x)
    return (jnp.dot(m, hi, preferred_element_type=F32)
            + (jnp.dot(m, mid, preferred_element_type=F32) + jnp.dot(m, lo, preferred_element_type=F32)))


def _rms(x, gain):
    return x * lax.rsqrt(jnp.mean(x * x, axis=-1, keepdims=True) + RMS_EPS) * gain


def _shift_prev(x, first_rows):
    n = x.shape[0]
    row = lax.broadcasted_iota(jnp.int32, x.shape, 0)
    out = []
    for k, edge in enumerate(first_rows, start=1):
        y = pltpu.roll(x, k, 0)
        for j in range(k):
            y = jnp.where(row == j, edge[j:j + 1], y)
        out.append(y)
    return out


def _shift_next(x, after_row):
    n = x.shape[0]
    row = lax.broadcasted_iota(jnp.int32, x.shape, 0)
    return jnp.where(row == n - 1, after_row, pltpu.roll(x, n - 1, 0))


def _halo_specs(ts, d):
    g = ts // SUBLANES

    def prev_map(b, i):
        return (b, jnp.maximum(i * g - 1, 0), 0, 0)

    def next_map(b, i, n_groups):
        return (b, jnp.minimum((i + 1) * g, n_groups - 1), 0, 0)

    return prev_map, next_map


def _rwkv_proj_kernel(x_ref, xp_ref, xn_ref, gain_ref, mu_ref, w0_ref, a0_ref,
                      wrkv_ref, w1_ref, w2_ref, a1_ref, a2_ref, g1_ref, g2_ref,
                      r_ref, k_ref, v_ref, a_fwd_ref, a_bwd_ref, lw_fwd_ref, lw_bwd_ref, g_ref):
    i = pl.program_id(1)
    last = pl.num_programs(1) - 1
    gain = gain_ref[...]
    h = _rms(x_ref[0], gain)
    h_before = jnp.where(i > 0, _rms(xp_ref[0, 0], gain)[SUBLANES - 1:SUBLANES], 0.0)
    h_after = jnp.where(i < last, _rms(xn_ref[0, 0], gain)[0:1], 0.0)
    (prev,) = _shift_prev(h, [h_before])
    nxt = _shift_next(h, h_after)
    xx = 0.5 * (prev + nxt) - h
    mu = mu_ref[...]

    def mix(p):
        return (h + xx * mu[p:p + 1]).astype(BF16)

    r_ref[0] = _dot(mix(0), wrkv_ref[0])
    k_ref[0] = _dot(mix(1), wrkv_ref[1])
    v_ref[0] = _dot(mix(2), wrkv_ref[2])

    tw = jnp.tanh(_dot(mix(3), w1_ref[...]))
    ta = _dot(mix(4), a1_ref[...])
    lane = lax.broadcasted_iota(jnp.int32, tw.shape, 1)
    for z, (lw_ref, a_ref) in enumerate(((lw_fwd_ref, a_fwd_ref), (lw_bwd_ref, a_bwd_ref))):
        mine = (lane // RW_HEAD) == z
        lora_w = _dot(jnp.where(mine, tw, 0.0), w2_ref[...])
        w_log = -jax.nn.softplus(-(w0_ref[z:z + 1] + lora_w)) - 0.5
        lw_ref[0] = -jnp.exp(w_log)
        lora_a = _dot(jnp.where(mine, ta, 0.0), a2_ref[...])
        a_ref[0] = jax.nn.sigmoid(a0_ref[z:z + 1] + lora_a)
    g_ref[0] = _dot(jax.nn.sigmoid(_dot(mix(5), g1_ref[...])), g2_ref[...])


def _rwkv_proj(x, gain, mu, w0, a0, wrkv, w1, w2, a1, a2, g1, g2, ts):
    b, s, d = x.shape
    nt = s // ts
    n_groups = s // SUBLANES
    x4 = x.reshape(b, n_groups, SUBLANES, d)
    prev_map, next_map = _halo_specs(ts, d)
    tile = pl.BlockSpec((1, ts, d), lambda bi, i: (bi, i, 0))
    halo = (1, 1, SUBLANES, d)

    def whole(a):
        return pl.BlockSpec(a.shape, lambda bi, i, nd=a.ndim: (0,) * nd)

    params = (gain, mu, w0, a0, wrkv, w1, w2, a1, a2, g1, g2)
    out = jax.ShapeDtypeStruct((b, s, d), F32)
    return pl.pallas_call(
        _rwkv_proj_kernel,
        grid=(b, nt),
        in_specs=[tile,
                  pl.BlockSpec(halo, prev_map),
                  pl.BlockSpec(halo, functools.partial(next_map, n_groups=n_groups))]
                 + [whole(p) for p in params],
        out_specs=[tile] * 8,
        out_shape=[out] * 8,
        compiler_params=_params("parallel", "parallel"),
        name="rwkv_proj",
    )(x, x4, x4, *params)


def _rwkv_scan_kernel(*refs, reverse, final, n_chunks):
    if final:
        (r_ref, k_ref, v_ref, a_ref, lw_ref, kk_scale_ref, ka_ref,
         yf_ref, a_other_ref, rk_ref, lng_ref, lnb_ref, y_ref, state_ref) = refs
    else:
        (r_ref, k_ref, v_ref, a_ref, lw_ref, kk_scale_ref, ka_ref, y_ref, state_ref) = refs

    @pl.when(pl.program_id(2) == 0)
    def _():
        state_ref[...] = jnp.zeros_like(state_ref)

    c, w2 = CHUNK, 2 * RW_HEAD
    row = lax.broadcasted_iota(jnp.int32, (c, w2), 0)
    lane = lax.broadcasted_iota(jnp.int32, (c, w2), 1)
    col = lane % RW_HEAD
    head0 = lane < RW_HEAD
    if reverse:
        strict, incl = col > row, col >= row
    else:
        strict, incl = col < row, col <= row
    leaf = (col // LEAF) == (row // LEAF)
    eye_pair = (col == row).astype(F32)
    rr = lax.broadcasted_iota(jnp.int32, (c, c), 0)
    cc = lax.broadcasted_iota(jnp.int32, (c, c), 1)
    cum_mat = ((cc >= rr) if reverse else (cc <= rr)).astype(F32)
    r2 = lax.broadcasted_iota(jnp.int32, (w2, w2), 0)
    c2 = lax.broadcasted_iota(jnp.int32, (w2, w2), 1)
    same_head = (r2 // RW_HEAD) == (c2 // RW_HEAD)
    head_ones = same_head.astype(F32)
    eye_w2 = r2 == c2

    def bd(m):
        return jnp.concatenate([jnp.where(head0, m, 0.0), jnp.where(head0, 0.0, m)], axis=0)

    def pm(a, m):
        return _dot(a, bd(m))

    n_pairs = r_ref.shape[-1] // w2
    order = list(range(n_chunks - 1, -1, -1) if reverse else range(n_chunks))
    items = [(pl.ds(j * c, c), slice(p * w2, (p + 1) * w2)) for p in range(n_pairs) for j in order]

    def each(fn, *lists):
        return [fn(*args) for args in zip(*lists)]

    def tile(ref):
        return [ref[0, rw, ls] for rw, ls in items]

    def vec(ref):
        return [ref[:, ls] for _, ls in items]

    r, k, v, a, lw = tile(r_ref), tile(k_ref), tile(v_ref), tile(a_ref), tile(lw_ref)
    ka = vec(ka_ref)
    kk = each(lambda x, sc: x * sc, k, vec(kk_scale_ref))
    sumsq = each(lambda x: _dot_f32_by_01(x * x, head_ones), kk)
    cum = each(lambda x: _dot_01_by_f32(cum_mat, x), lw)
    kap = each(lambda x, ss: x / jnp.maximum(jnp.sqrt(ss), 1e-12), kk, sumsq)
    kd = each(lambda x, aa, kav: x * (1.0 + (aa - 1.0) * kav), k, a, ka)
    beta = each(lambda aa, kp: aa * kp, a, kap)
    total = each(lambda x: x[0:1] if reverse else x[c - 1:c], cum)
    g_out = each(lambda x: jnp.exp(-x), cum)
    g_tail = each(lambda t, x: jnp.exp(t - x), total, cum)
    r_s = each(lambda x, cm: x * jnp.exp(cm), r, cum)
    kap_s = each(lambda x, cm, l: x * jnp.exp(cm - l), kap, cum, lw)
    k_s = each(lambda x, g: x * g, kd, g_out)
    b_s = each(lambda x, g: x * g, beta, g_out)
    k_hat = each(lambda x, g: x * g, kd, g_tail)
    b_hat = each(lambda x, g: x * g, beta, g_tail)

    def gram_fn(kp, rs, ks, bs):
        stacked = jnp.concatenate([jnp.where(head0, ks, 0.0), jnp.where(head0, 0.0, ks),
                                   jnp.where(head0, bs, 0.0), jnp.where(head0, 0.0, bs)], axis=0)
        return _dot_nt(jnp.concatenate([kp, rs], axis=0), stacked)

    gram = each(gram_fn, kap_s, r_s, k_s, b_s)
    a_kk = each(lambda g: jnp.where(strict, g[:c, :w2], 0.0), gram)
    n_mat = each(lambda g: jnp.where(strict, g[:c, w2:], 0.0), gram)
    a_rk = each(lambda g: jnp.where(incl, g[c:, :w2], 0.0), gram)
    a_rb = each(lambda g: jnp.where(incl, g[c:, w2:], 0.0), gram)

    n_d = each(lambda n: jnp.where(leaf, n, 0.0), n_mat)
    n_off = each(lambda n, nd: n - nd, n_mat, n_d)
    n2 = each(pm, n_d, n_d)
    n4 = each(pm, n2, n2)
    n8 = each(pm, n4, n4)
    t_d = each(lambda x, y: pm(eye_pair - x, eye_pair + y), n_d, n2)
    t_d = each(lambda x, y: pm(x, eye_pair + y), t_d, n4)
    t_d = each(lambda x, y: pm(x, eye_pair + y), t_d, n8)
    e1 = each(pm, t_d, n_off)
    e2 = each(pm, e1, e1)
    t_mat = each(lambda x, y: pm(eye_pair - x, eye_pair + y), e1, e2)
    t_mat = each(pm, t_mat, t_d)

    v_bd = each(bd, v)
    av = each(_dot, a_kk, v_bd)
    mw = each(lambda t, kp, x: _dot(t, jnp.concatenate([bd(kp), bd(x)], axis=1)), t_mat, kap_s, av)
    m_mat = each(lambda x: x[:, :w2], mw)
    w_mat = each(lambda x: x[:, w2:], mw)
    zeros_bd = jnp.zeros((w2, w2), F32)

    def qy_fn(ark, arb, vb, m, w):
        return _dot(jnp.concatenate([ark, arb], axis=1),
                    jnp.concatenate([jnp.concatenate([zeros_bd, vb], axis=1),
                                     jnp.concatenate([-bd(m), -bd(w)], axis=1)], axis=0))

    def ph_fn(kh, bh, vv, m, w):
        return _dot_tn(jnp.concatenate([kh, bh], axis=0),
                       jnp.concatenate([jnp.concatenate([jnp.zeros_like(vv), vv], axis=1),
                                        jnp.concatenate([-m, -w], axis=1)], axis=0))

    qy = each(qy_fn, a_rk, a_rb, v_bd, m_mat, w_mat)
    ph = each(ph_fn, k_hat, b_hat, v, m_mat, w_mat)
    q_mat = each(lambda rs, x: rs + x[:, :w2], r_s, qy)
    y_intra = each(lambda x: x[:, w2:], qy)
    p_mat = each(lambda x, t: jnp.where(same_head, x[:, :w2], 0.0) + jnp.where(eye_w2, jnp.exp(t), 0.0),
                 ph, total)
    h_intra = each(lambda x: jnp.where(same_head, x[:, w2:], 0.0), ph)

    y = []
    for p in range(n_pairs):
        state = state_ref[p]
        for i in range(p * n_chunks, (p + 1) * n_chunks):
            step = _dot(jnp.concatenate([q_mat[i], p_mat[i]], axis=0), state)
            y.append(step[:c] + y_intra[i])
            state = step[c:] + h_intra[i]
        state_ref[p] = state

    if final:
        y = each(lambda x, yf: x + yf, y, tile(yf_ref))
        mean = each(lambda x: _dot_f32_by_01(x, head_ones) * (1.0 / RW_HEAD), y)
        yc = each(lambda x, m: x - m, y, mean)
        var = each(lambda x: _dot_f32_by_01(x * x, head_ones) * (1.0 / RW_HEAD), yc)
        kd_other = each(lambda x, ao, kav: x * (1.0 + (ao - 1.0) * kav), k, tile(a_other_ref), ka)
        bonus = each(lambda rr_, x, xo, rk: _dot_f32_by_01(rr_ * (x + xo) * rk, head_ones),
                     r, kd, kd_other, vec(rk_ref))
        y = each(lambda x, vr, bo, vv, lg, lb: x * lax.rsqrt(vr + RW_LNX_EPS) * lg + lb + bo * vv,
                 yc, var, bonus, v, vec(lng_ref), vec(lnb_ref))
    for (rw, ls), x in zip(items, y):
        y_ref[0, rw, ls] = x


def _rwkv_scan(r, k, v, a, lw, kk_scale, ka, extra, *, reverse, ts):
    b, s, d = r.shape
    nt = s // ts
    w2 = 2 * RW_HEAD
    wb = RW_SCAN_PAIRS * w2
    if reverse:
        tile = pl.BlockSpec((1, ts, wb), lambda bi, hp, i: (bi, nt - 1 - i, hp))
    else:
        tile = pl.BlockSpec((1, ts, wb), lambda bi, hp, i: (bi, i, hp))
    vec = pl.BlockSpec((1, wb), lambda bi, hp, i: (0, hp))
    final = extra is not None
    args = [r, k, v, a, lw, kk_scale, ka]
    specs = [tile] * 5 + [vec] * 2
    if final:
        yf, a_other, rk, lng, lnb = extra
        args += [yf, a_other, rk, lng, lnb]
        specs += [tile] * 2 + [vec] * 3
    return pl.pallas_call(
        functools.partial(_rwkv_scan_kernel, reverse=reverse, final=final, n_chunks=ts // CHUNK),
        grid=(b, d // wb, nt),
        in_specs=specs,
        out_specs=tile,
        out_shape=jax.ShapeDtypeStruct((b, s, d), F32),
        scratch_shapes=[pltpu.VMEM((RW_SCAN_PAIRS, w2, w2), F32)],
        compiler_params=_params("parallel", "parallel", "arbitrary"),
        name="rwkv_scan_bwd" if reverse else "rwkv_scan_fwd",
    )(*args)


def _gated_out_kernel(x_ref, y_ref, g_ref, w_ref, o_ref):
    o_ref[...] = x_ref[...] + _dot(y_ref[...] * g_ref[...], w_ref[...])


def _gated_out(x, y, g, w, tm):
    t, d = x.shape
    tile = pl.BlockSpec((tm, d), lambda i: (i, 0))
    return pl.pallas_call(
        _gated_out_kernel,
        grid=(t // tm,),
        in_specs=[tile, tile, tile, pl.BlockSpec(w.shape, lambda i: (0, 0))],
        out_specs=tile,
        out_shape=jax.ShapeDtypeStruct((t, d), F32),
        compiler_params=_params("parallel"),
        name="gated_out",
    )(x, y, g, w)


def _sconv_kernel(x_ref, xp_ref, xn_ref, gain_ref, win_ref, cw_ref, wout_ref, o_ref):
    i = pl.program_id(1)
    last = pl.num_programs(1) - 1
    d = x_ref.shape[-1]
    gain = gain_ref[...]
    x = x_ref[0]
    bcu = _dot(_rms(x, gain), win_ref[...])
    cu = bcu[:, d:2 * d] * bcu[:, 2 * d:]

    def edge_cu(ref):
        e = _dot(_rms(ref[0, 0], gain), win_ref[:, d:])
        return e[:, :d] * e[:, d:]

    cu_before = jnp.where(i > 0, edge_cu(xp_ref)[SUBLANES - 1:SUBLANES], 0.0)
    cu_after = jnp.where(i < last, edge_cu(xn_ref)[0:1], 0.0)
    (prev,) = _shift_prev(cu, [cu_before])
    nxt = _shift_next(cu, cu_after)
    cw = cw_ref[...]
    z = cw[0:1] * prev + cw[1:2] * cu + cw[2:3] * nxt
    o_ref[0] = x + _dot(bcu[:, :d] * z, wout_ref[...])


def _sconv_mixer(x, gain, w_in, conv_w, w_out, ts):
    b, s, d = x.shape
    n_groups = s // SUBLANES
    x4 = x.reshape(b, n_groups, SUBLANES, d)
    prev_map, next_map = _halo_specs(ts, d)
    tile = pl.BlockSpec((1, ts, d), lambda bi, i: (bi, i, 0))
    halo = (1, 1, SUBLANES, d)

    def whole(a):
        return pl.BlockSpec(a.shape, lambda bi, i, nd=a.ndim: (0,) * nd)

    params = (gain, w_in, conv_w, w_out)
    return pl.pallas_call(
        _sconv_kernel,
        grid=(b, s // ts),
        in_specs=[tile, pl.BlockSpec(halo, prev_map),
                  pl.BlockSpec(halo, functools.partial(next_map, n_groups=n_groups))]
                 + [whole(p) for p in params],
        out_specs=tile,
        out_shape=jax.ShapeDtypeStruct((b, s, d), F32),
        compiler_params=_params("parallel", "parallel"),
        name="sconv_mixer",
    )(x, x4, x4, *params)


def _tile_scan(a, b, reverse):
    n = a.shape[0]
    row = lax.broadcasted_iota(jnp.int32, a.shape, 0)
    s = 1
    while s < n:
        if reverse:
            a_far, b_far, valid = pltpu.roll(a, n - s, 0), pltpu.roll(b, n - s, 0), row < n - s
        else:
            a_far, b_far, valid = pltpu.roll(a, s, 0), pltpu.roll(b, s, 0), row >= s
        b = jnp.where(valid, a * b_far + b, b)
        a = jnp.where(valid, a * a_far, a)
        s *= 2
    return a, b


def _lru_gates(u, wa_ref, wx_ref, ba, bx, lam):
    d = u.shape[-1]
    blk = d // LRU_HEADS
    ub = u.astype(BF16)
    r_parts, i_parts = [], []
    for hh in range(LRU_HEADS):
        uh = ub[:, hh * blk:(hh + 1) * blk]
        r_parts.append(jnp.dot(uh, wa_ref[hh], preferred_element_type=F32))
        i_parts.append(jnp.dot(uh, wx_ref[hh], preferred_element_type=F32))
    r_gate = jax.nn.sigmoid(jnp.concatenate(r_parts, axis=1) + ba)
    i_gate = jax.nn.sigmoid(jnp.concatenate(i_parts, axis=1) + bx)
    log_a = -LRU_C * r_gate * jax.nn.softplus(-lam)
    th = jnp.tanh(log_a)
    return jnp.exp(log_a), jnp.sqrt(-2.0 * th / (1.0 - th)) * (i_gate * u)


def _lru_fwd_kernel(x_ref, xp_ref, xn_ref, gain_ref, win_ref, cw_ref, cb_ref,
                    wa_ref, wx_ref, ba_ref, bx_ref, lam_ref,
                    hf_ref, u_ref, gate_ref, carry_ref):
    i = pl.program_id(1)
    last = pl.num_programs(1) - 1
    d = x_ref.shape[-1]
    gain = gain_ref[...]

    @pl.when(i == 0)
    def _():
        carry_ref[...] = jnp.zeros_like(carry_ref)

    gu = _dot(_rms(x_ref[0], gain), win_ref[...])
    gate_ref[0] = jax.nn.gelu(gu[:, :d])
    u_raw = gu[:, d:]
    before = jnp.where(i > 0, _dot(_rms(xp_ref[0, 0], gain), win_ref[:, d:]), 0.0)
    after = jnp.where(i < last, _dot(_rms(xn_ref[0, 0], gain), win_ref[:, d:]), 0.0)
    prev1, prev2 = _shift_prev(u_raw, [before[SUBLANES - 1:], before[SUBLANES - 2:]])
    nxt = _shift_next(u_raw, after[0:1])
    cw = cw_ref[...]
    u = cw[0:1] * prev2 + cw[1:2] * prev1 + cw[2:3] * u_raw + cw[3:4] * nxt + cb_ref[...]
    u_ref[0] = u
    a, b = _lru_gates(u, wa_ref, wx_ref, ba_ref[...], bx_ref[...], lam_ref[...])
    a_run, h_loc = _tile_scan(a, b, reverse=False)
    hf = h_loc + a_run * carry_ref[...]
    hf_ref[0] = hf
    carry_ref[...] = hf[hf.shape[0] - 1:]


def _lru_bwd_kernel(x_ref, u_ref, gate_ref, hf_ref, wa_ref, wx_ref, ba_ref, bx_ref, lam_ref, wout_ref,
                    o_ref, carry_ref):
    @pl.when(pl.program_id(1) == 0)
    def _():
        carry_ref[...] = jnp.zeros_like(carry_ref)

    a, b = _lru_gates(u_ref[0], wa_ref, wx_ref, ba_ref[...], bx_ref[...], lam_ref[...])
    a_run, h_loc = _tile_scan(a, b, reverse=True)
    hb = h_loc + a_run * carry_ref[...]
    carry_ref[...] = hb[0:1]
    o_ref[0] = x_ref[0] + _dot((hf_ref[0] + hb) * gate_ref[0], wout_ref[...])


def _lru_mixer(x, gain, w_in, conv_w, conv_b, wa, wx, ba, bx, lam, w_out, ts):
    b, s, d = x.shape
    nt = s // ts
    n_groups = s // SUBLANES
    x4 = x.reshape(b, n_groups, SUBLANES, d)
    prev_map, next_map = _halo_specs(ts, d)
    tile = pl.BlockSpec((1, ts, d), lambda bi, i: (bi, i, 0))
    rtile = pl.BlockSpec((1, ts, d), lambda bi, i: (bi, nt - 1 - i, 0))
    halo = (1, 1, SUBLANES, d)

    def whole(a):
        return pl.BlockSpec(a.shape, lambda bi, i, nd=a.ndim: (0,) * nd)

    out = jax.ShapeDtypeStruct((b, s, d), F32)
    fwd_params = (gain, w_in, conv_w, conv_b, wa[0], wx[0], ba[0:1], bx[0:1], lam[0:1])
    hf, u, gate = pl.pallas_call(
        _lru_fwd_kernel,
        grid=(b, nt),
        in_specs=[tile, pl.BlockSpec(halo, prev_map),
                  pl.BlockSpec(halo, functools.partial(next_map, n_groups=n_groups))]
                 + [whole(p) for p in fwd_params],
        out_specs=[tile] * 3,
        out_shape=[out] * 3,
        scratch_shapes=[pltpu.VMEM((1, d), F32)],
        compiler_params=_params("parallel", "arbitrary"),
        name="lru_fwd",
    )(x, x4, x4, *fwd_params)
    bwd_params = (wa[1], wx[1], ba[1:2], bx[1:2], lam[1:2], w_out)
    return pl.pallas_call(
        _lru_bwd_kernel,
        grid=(b, nt),
        in_specs=[rtile] * 4 + [whole(p) for p in bwd_params],
        out_specs=rtile,
        out_shape=out,
        scratch_shapes=[pltpu.VMEM((1, d), F32)],
        compiler_params=_params("parallel", "arbitrary"),
        name="lru_bwd",
    )(x, u, gate, hf, *bwd_params)


def _route_kernel(x_ref, gain_ref, rw_ref, xin_ref, gcol_ref, rank_ref, h_ref, rank_scr, gate_scr, *, cap):
    e = pl.program_id(1)
    n_exp, s = rank_scr.shape

    @pl.when(e == 0)
    def _():
        h = _rms(x_ref[0], gain_ref[...])
        h_ref[...] = h.astype(BF16)
        logits = lax.dot_general(rw_ref[...], h, (((1,), (1,)), ((), ())),
                                 preferred_element_type=F32, precision=HIGHEST)
        ex = jnp.exp(logits - jnp.max(logits, axis=0, keepdims=True))
        aff = ex / jnp.sum(ex, axis=0, keepdims=True)
        def count(pred):
            return jnp.sum(jnp.where(pred, 1.0, 0.0), axis=1, keepdims=True)

        def binade_step(_, jj):
            j_fail, j_ok = jj
            j_mid = jnp.floor(0.5 * (j_fail + j_ok))
            ok = count(aff >= jnp.exp2(1.0 - j_mid)) >= cap
            return jnp.where(ok, j_fail, j_mid), jnp.where(ok, j_mid, j_ok)

        _, j_ok = lax.fori_loop(0, 8, binade_step,
                                (jnp.full((n_exp, 1), -1.0, F32), jnp.full((n_exp, 1), 255.0, F32)))

        def value_step(_, lohi):
            lo, hi = lohi
            mid = 0.5 * (lo + hi)
            ok = count(aff >= mid) >= cap
            return jnp.where(ok, mid, lo), jnp.where(ok, hi, mid)

        lo, hi = lax.fori_loop(0, VALUE_BISECT_STEPS, value_step,
                               (jnp.exp2(1.0 - j_ok), jnp.exp2(2.0 - j_ok)))
        above = aff >= hi
        tied = (aff >= lo) & jnp.logical_not(above)
        need = cap - count(above)
        idx = lax.broadcasted_iota(jnp.int32, (n_exp, s), 1).astype(F32)

        def index_step(_, mm):
            m_fail, m_ok = mm
            m_mid = jnp.floor(0.5 * (m_fail + m_ok))
            ok = count(tied & (idx <= m_mid)) >= need
            return jnp.where(ok, m_fail, m_mid), jnp.where(ok, m_mid, m_ok)

        _, m_ok = lax.fori_loop(0, (s - 1).bit_length(), index_step,
                                (jnp.full((n_exp, 1), -1.0, F32), jnp.full((n_exp, 1), s - 1.0, F32)))
        chosen = above | (tied & (idx <= m_ok))
        gate_scr[...] = jnp.where(chosen, aff, 0.0)
        sel = chosen.astype(F32)
        ri = lax.broadcasted_iota(jnp.int32, (LANES, LANES), 0)
        ci = lax.broadcasted_iota(jnp.int32, (LANES, LANES), 1)
        prefix = (ri <= ci).astype(BF16)
        seen = jnp.zeros((n_exp, 1), F32)
        for j in range(s // LANES):
            sj = sel[:, j * LANES:(j + 1) * LANES]
            pos = jnp.dot(sj.astype(BF16), prefix, preferred_element_type=F32) + seen
            rank_scr[:, j * LANES:(j + 1) * LANES] = jnp.where(sj > 0.0, pos, 0.0)
            seen = seen + jnp.sum(sj, axis=1, keepdims=True)

    rank_row = rank_scr[pl.ds(e, 1), :]
    slot = (lax.broadcasted_iota(jnp.int32, (cap, s), 0) + 1).astype(F32)
    hit = rank_row == slot
    xin_ref[0, 0] = jnp.dot(jnp.where(hit, 1.0, 0.0).astype(BF16), h_ref[...],
                            preferred_element_type=F32).astype(BF16)
    gcol_ref[0, 0] = jnp.sum(jnp.where(hit, gate_scr[pl.ds(e, 1), :], 0.0), axis=1, keepdims=True)
    rank_ref[0, 0] = rank_row


def _route(x, gain, router_wt, cap):
    b, s, d = x.shape
    n_exp = router_wt.shape[0]
    return pl.pallas_call(
        functools.partial(_route_kernel, cap=cap),
        grid=(b, n_exp),
        in_specs=[pl.BlockSpec((1, s, d), lambda bi, e: (bi, 0, 0)),
                  pl.BlockSpec((1, d), lambda bi, e: (0, 0)),
                  pl.BlockSpec((n_exp, d), lambda bi, e: (0, 0))],
        out_specs=[pl.BlockSpec((1, 1, cap, d), lambda bi, e: (bi, e, 0, 0)),
                   pl.BlockSpec((1, 1, cap, 1), lambda bi, e: (bi, e, 0, 0)),
                   pl.BlockSpec((1, 1, 1, s), lambda bi, e: (bi, e, 0, 0))],
        out_shape=[jax.ShapeDtypeStruct((b, n_exp, cap, d), BF16),
                   jax.ShapeDtypeStruct((b, n_exp, cap, 1), F32),
                   jax.ShapeDtypeStruct((b, n_exp, 1, s), F32)],
        scratch_shapes=[pltpu.VMEM((s, d), BF16), pltpu.VMEM((n_exp, s), F32), pltpu.VMEM((n_exp, s), F32)],
        compiler_params=_params("parallel", "arbitrary"),
        name="moe_route",
    )(x, gain, router_wt)


def _ffn_kernel(xin_ref, gcol_ref, wg_ref, wu_ref, wd_ref, o_ref, acc_ref):
    f = pl.program_id(1)
    b, _, cap, d = xin_ref.shape
    xin = xin_ref[...].reshape(b * cap, d)
    hid = jax.nn.silu(_dot(xin, wg_ref[0, 0])) * _dot(xin, wu_ref[0, 0])
    part = _dot(hid, wd_ref[0, 0])

    @pl.when(f == 0)
    def _():
        acc_ref[...] = part

    @pl.when(f > 0)
    def _():
        acc_ref[...] += part

    @pl.when(f == pl.num_programs(1) - 1)
    def _():
        o_ref[...] = (acc_ref[...] * gcol_ref[...].reshape(b * cap, 1)).astype(BF16).reshape(o_ref.shape)


def _ffn(xin, gcol, w_gate, w_up, w_down, layer, tf):
    b, n_exp, cap, d = xin.shape
    ff = w_gate.shape[-1]
    tok = pl.BlockSpec((b, 1, cap, d), lambda e, f: (0, e, 0, 0))
    return pl.pallas_call(
        _ffn_kernel,
        grid=(n_exp, ff // tf),
        in_specs=[tok,
                  pl.BlockSpec((b, 1, cap, 1), lambda e, f: (0, e, 0, 0)),
                  pl.BlockSpec((1, 1, d, tf), lambda e, f: (layer, e, 0, f)),
                  pl.BlockSpec((1, 1, d, tf), lambda e, f: (layer, e, 0, f)),
                  pl.BlockSpec((1, 1, tf, d), lambda e, f: (layer, e, f, 0))],
        out_specs=tok,
        out_shape=jax.ShapeDtypeStruct(xin.shape, BF16),
        scratch_shapes=[pltpu.VMEM((b * cap, d), F32)],
        compiler_params=_params("parallel", "arbitrary"),
        name="moe_ffn",
    )(xin, gcol, w_gate, w_up, w_down)


def _combine_kernel(x_ref, rank_ref, y_ref, gain_ref, o_ref, *, final_norm):
    e = pl.program_id(2)
    cap = y_ref.shape[2]
    ts = o_ref.shape[1]

    @pl.when(e == 0)
    def _():
        o_ref[...] = x_ref[...]

    slot = (lax.broadcasted_iota(jnp.int32, (cap, ts), 0) + 1).astype(F32)
    onehot = jnp.where(rank_ref[0, 0] == slot, 1.0, 0.0)
    o_ref[0] += _dot_tn(onehot, y_ref[0, 0])

    if final_norm:
        @pl.when(e == pl.num_programs(2) - 1)
        def _():
            o_ref[0] = _rms(o_ref[0], gain_ref[...])


def _combine(x, rank, yexp, gain, ts, final_norm):
    b, s, d = x.shape
    n_exp, cap = yexp.shape[1], yexp.shape[2]
    tile = pl.BlockSpec((1, ts, d), lambda bi, i, e: (bi, i, 0))
    return pl.pallas_call(
        functools.partial(_combine_kernel, final_norm=final_norm),
        grid=(b, s // ts, n_exp),
        in_specs=[tile,
                  pl.BlockSpec((1, 1, 1, ts), lambda bi, i, e: (bi, e, 0, i)),
                  pl.BlockSpec((1, 1, cap, d), lambda bi, i, e: (bi, e, 0, 0)),
                  pl.BlockSpec((1, d), lambda bi, i, e: (0, 0))],
        out_specs=tile,
        out_shape=jax.ShapeDtypeStruct((b, s, d), F32),
        compiler_params=_params("parallel", "parallel", "arbitrary"),
        name="moe_combine",
    )(x, rank, yexp, gain)


def _moe(x, gain, router_w, w_gate, w_up, w_down, layer, final_gain):
    b, s, d = x.shape
    cap = CAPACITY_FACTOR * s // N_EXPERTS
    xin, gcol, rank = _route(x, gain, router_w.T, cap)
    yexp = _ffn(xin, gcol, w_gate, w_up, w_down, layer, tf=512)
    final_norm = final_gain is not None
    return _combine(x, rank, yexp, final_gain if final_norm else gain, ts=s // 2, final_norm=final_norm)


def _rwkv_layer(x, gain, mu, w_in, w0, w1, w2, a0, a1, a2, g1, g2, k_k, k_a, r_k, lnx_g, lnx_b, w_out):
    b, s, d = x.shape
    row = lambda t: t.reshape(1, d)
    mu8 = jnp.concatenate([mu, jnp.zeros((SUBLANES - mu.shape[0], d), F32)], axis=0)
    cat_in = lambda t: jnp.concatenate([t[0], t[1]], axis=1).astype(BF16)
    cat_out = lambda t: jnp.concatenate([t[0], t[1]], axis=0).astype(BF16)
    r, k, v, a_f, a_b, lw_f, lw_b, g = _rwkv_proj(
        x, row(gain), mu8, w0, a0, w_in.astype(BF16), cat_in(w1), cat_out(w2), cat_in(a1), cat_out(a2),
        g1.astype(BF16), g2.astype(BF16), ts=256)
    y_f = _rwkv_scan(r, k, v, a_f, lw_f, row(k_k), row(k_a), None, reverse=False, ts=RW_SCAN_TILE)
    y = _rwkv_scan(r, k, v, a_b, lw_b, row(k_k), row(k_a),
                   (y_f, a_f, row(r_k), row(lnx_g), row(lnx_b)), reverse=True, ts=RW_SCAN_TILE)
    t = b * s
    out = _gated_out(x.reshape(t, d), y.reshape(t, d), g.reshape(t, d), w_out.astype(BF16), tm=512)
    return out.reshape(b, s, d)


def kernel(x, norm_mix_g, norm_ffn_g, router_w, exp_w_gate, exp_w_up, exp_w_down, final_norm_g, rw_mu, rw_w_in, rw_w0, rw_w1, rw_w2, rw_a0, rw_a1, rw_a2, rw_g1, rw_g2, rw_k_k, rw_k_a, rw_r_k, rw_lnx_g, rw_lnx_b, rw_w_out, sc_w_in, sc_conv_w, sc_w_out, lru_w_in, lru_conv_w, lru_conv_b, lru_w_a, lru_b_a, lru_w_x, lru_b_x, lru_lambda, lru_w_out):
    depth, d = norm_mix_g.shape
    ia = ib = ic = 0
    for i in range(depth):
        gain = norm_mix_g[i].reshape(1, d)
        m = i % 3
        if m == 0:
            x = _rwkv_layer(x, norm_mix_g[i], rw_mu[ia], rw_w_in[ia], rw_w0[ia], rw_w1[ia], rw_w2[ia],
                            rw_a0[ia], rw_a1[ia], rw_a2[ia], rw_g1[ia], rw_g2[ia], rw_k_k[ia], rw_k_a[ia],
                            rw_r_k[ia], rw_lnx_g[ia], rw_lnx_b[ia], rw_w_out[ia])
            ia += 1
        elif m == 1:
            cw = jnp.concatenate([sc_conv_w[ib], jnp.zeros((SUBLANES - sc_conv_w.shape[1], d), F32)], axis=0)
            x = _sconv_mixer(x, gain, sc_w_in[ib].astype(BF16), cw, sc_w_out[ib].astype(BF16), ts=256)
            ib += 1
        else:
            cw = jnp.concatenate([lru_conv_w[ic], jnp.zeros((SUBLANES - lru_conv_w.shape[1], d), F32)], axis=0)
            x = _lru_mixer(x, gain, lru_w_in[ic].astype(BF16), cw, lru_conv_b[ic].reshape(1, d),
                           lru_w_a[ic].astype(BF16), lru_w_x[ic].astype(BF16), lru_b_a[ic], lru_b_x[ic],
                           lru_lambda[ic], lru_w_out[ic].astype(BF16), ts=256)
            ic += 1
        x = _moe(x, norm_ffn_g[i].reshape(1, d), router_w[i], exp_w_gate, exp_w_up, exp_w_down, i,
                 final_norm_g.reshape(1, d) if i == depth - 1 else None)
    return x
```
